```python
import numpy as np
import jax
import jax.numpy as jnp
from jax import lax

D_MODEL = 1024
BATCH = 16
SEQ = 2048
DEPTH = 2

EPS = 1e-6
NEG_BIG = -1e30
TINY = 1e-30
D_FF = ((8 * D_MODEL // 3 + 127) // 128) * 128
N_SUB = 3
N_BRANCH = 3

ML_WIDTH = D_MODEL
ML_HEADS = 4
ML_HD = ML_WIDTH // ML_HEADS
ML_CONV = 4
ML_CHUNK = 64

RW_WIDTH = D_MODEL
RW_HEAD = 64
RW_HEADS = RW_WIDTH // RW_HEAD
RW_DECAY_LORA = 64
RW_AAA_LORA = 64
RW_MV_LORA = 32
RW_GATE_LORA = 128
RW_LN_EPS = 64e-5
RW_SPLITS = (RW_WIDTH, RW_WIDTH, RW_WIDTH, RW_DECAY_LORA, RW_AAA_LORA, RW_GATE_LORA)
RW_SHIFT_COLS = 3 * RW_WIDTH + RW_DECAY_LORA + RW_AAA_LORA + RW_GATE_LORA

HG_WIDTH = D_MODEL
HG_EXPAND = 128
HG_HEADS = HG_WIDTH // HG_EXPAND
HG_CHUNK = 32

IN_SPLITS = (ML_WIDTH, ML_WIDTH, ML_HEADS, ML_HEADS, RW_SHIFT_COLS,
             HG_WIDTH, HG_WIDTH, HG_WIDTH, HG_WIDTH, N_BRANCH * D_MODEL)
N_IN = sum(IN_SPLITS)

kernel_name = 'hybrid_mlstm_rwkv7_hgrn2_macaron_adaln'


def rms_norm(x, w, eps=EPS):
    xf = x.astype(jnp.float32)
    y = xf * lax.rsqrt(jnp.mean(xf * xf, axis=-1, keepdims=True) + eps)
    return (y * w).astype(x.dtype)


def head_norm(x, n_heads, w, b=None, eps=EPS, center=False):
    shp = x.shape
    xf = x.astype(jnp.float32).reshape(*shp[:-1], n_heads, shp[-1] // n_heads)
    if center:
        xf = xf - jnp.mean(xf, axis=-1, keepdims=True)
    xf = xf * lax.rsqrt(jnp.mean(xf * xf, axis=-1, keepdims=True) + eps)
    y = xf.reshape(shp) * w
    if b is not None:
        y = y + b
    return y.astype(x.dtype)


def modulate(x, w, shift, scale):
    return rms_norm(x, w) * (1.0 + scale) + shift


def swiglu(h, w_up, w_down):
    a, b = jnp.split(h @ w_up, 2, axis=-1)
    return (jax.nn.silu(a) * b) @ w_down


def split_cols(z, sizes):
    return jnp.split(z, np.cumsum(sizes)[:-1].tolist(), axis=-1)


def split_heads(x, n):
    B, T, W = x.shape
    return x.reshape(B, T, n, W // n).transpose(0, 2, 1, 3)


def merge_heads(y):
    B, H, T, d = y.shape
    return y.transpose(0, 2, 1, 3).reshape(B, T, H * d)


def to_chunks(x, L):
    B, H, T = x.shape[:3]
    return jnp.moveaxis(x.reshape(B, H, T // L, L, *x.shape[3:]), 2, 0)


def from_chunks(y):
    y = jnp.moveaxis(y, 0, 2)
    return y.reshape(y.shape[0], y.shape[1], -1, y.shape[-1])


def token_shift(z, mu):
    prev = jnp.pad(z, ((0, 0), (1, 0), (0, 0)))[:, :-1]
    return z + mu * (prev - z)


def causal_conv(x, w, b):
    K, C = w.shape
    y = lax.conv_general_dilated(x, w.astype(x.dtype)[:, None, :], window_strides=(1,),
                                 padding=((K - 1, 0),), dimension_numbers=('NWC', 'WIO', 'NWC'),
                                 feature_group_count=C)
    return y + b


def mlstm_chunkwise(q, k, v, log_i, log_f):
    f32 = jnp.float32
    B, H, T, dk = q.shape
    dv = v.shape[-1]
    L = ML_CHUNK
    q, k, v, log_i, log_f = (t.astype(f32) for t in (q, k, v, log_i, log_f))
    k = k * (dk ** -0.5)
    causal = jnp.tril(jnp.ones((L, L), dtype=bool))

    def step(carry, inp):
        C, n, m = carry
        qc, kc, vc, li, lf = inp
        b = jnp.cumsum(lf, axis=-1)
        log_d = jnp.where(causal, b[..., :, None] - b[..., None, :] + li[..., None, :], NEG_BIG)
        log_inter = b + m[..., None]
        m_t = jnp.maximum(log_inter, jnp.max(log_d, axis=-1))
        w_intra = jnp.exp(log_d - m_t[..., None])
        w_inter = jnp.exp(log_inter - m_t)
        s = jnp.einsum('bhtd,bhsd->bhts', qc, kc) * w_intra
        num = (w_inter[..., None] * jnp.einsum('bhtd,bhde->bhte', qc, C)
               + jnp.einsum('bhts,bhse->bhte', s, vc))
        den = w_inter * jnp.einsum('bhtd,bhd->bht', qc, n) + jnp.sum(s, axis=-1)
        h = num / jnp.maximum(jnp.abs(den), jnp.exp(-m_t))[..., None]
        b_end = b[..., -1]
        log_ws = b_end[..., None] - b + li
        m_new = jnp.maximum(b_end + m, jnp.max(log_ws, axis=-1))
        ws = jnp.exp(log_ws - m_new[..., None])
        decay = jnp.exp(b_end + m - m_new)
        kw = kc * ws[..., None]
        C = decay[..., None, None] * C + jnp.einsum('bhsd,bhse->bhde', kw, vc)
        n = decay[..., None] * n + jnp.sum(kw, axis=2)
        return (C, n, m_new), h

    init = (jnp.zeros((B, H, dk, dv), f32), jnp.zeros((B, H, dk), f32), jnp.zeros((B, H), f32))
    xs = (to_chunks(q, L), to_chunks(k, L), to_chunks(v, L), to_chunks(log_i, L), to_chunks(log_f, L))
    _, h = lax.scan(step, init, xs)
    return from_chunks(h)


def mlstm_branch(x_m, o_pre, i_pre, f_pre, conv_w, conv_b, wq, wk, i_b, f_b, norm_w, skip):
    xc = jax.nn.silu(causal_conv(x_m, conv_w, conv_b))
    xc_h = split_heads(xc, ML_HEADS)
    q = jnp.einsum('bhtd,hde->bhte', xc_h, wq)
    k = jnp.einsum('bhtd,hde->bhte', xc_h, wk)
    v = split_heads(x_m, ML_HEADS)
    log_i = jnp.swapaxes(i_pre + i_b, 1, 2)
    log_f = jax.nn.log_sigmoid(jnp.swapaxes(f_pre + f_b, 1, 2).astype(jnp.float32))
    h = merge_heads(mlstm_chunkwise(q, k, v, log_i, log_f)).astype(x_m.dtype)
    return head_norm(h * jax.nn.sigmoid(o_pre), ML_HEADS, norm_w, center=True) + skip * xc


def wkv7_scan(r, log_w, k, v, kk, a):
    T, B, H, N = r.shape

    def step(S, inp):
        r_t, lw_t, k_t, v_t, kk_t, a_t = inp
        sa = jnp.einsum('bhvk,bhk->bhv', S, -kk_t)
        S = (S * jnp.exp(lw_t)[:, :, None, :]
             + sa[..., None] * (kk_t * a_t)[:, :, None, :]
             + v_t[..., None] * k_t[:, :, None, :])
        return S, jnp.einsum('bhvk,bhk->bhv', S, r_t)

    _, y = lax.scan(step, jnp.zeros((B, H, N, N), jnp.float32), (r, log_w, k, v, kk, a))
    return y


def rwkv7_branch(feats, vres, v_first, mu, w0, w2, a0, a2, g2, k_k, k_a, r_k, ln_w, ln_b):
    B, T, _ = feats.shape
    f32 = jnp.float32
    r, k, v, w_lo, a_lo, g_lo = split_cols(token_shift(feats, mu), RW_SPLITS)
    w_raw = -jax.nn.softplus(-(w0 + jnp.tanh(w_lo) @ w2).astype(f32)) - 0.5
    log_w = -jnp.exp(w_raw)
    a = jax.nn.sigmoid(a0 + a_lo @ a2)
    g = jax.nn.sigmoid(g_lo) @ g2
    if vres is None:
        v_first = v
    else:
        v_lo, mu_v, v0, v2 = vres
        v = v + (v_first - v) * jax.nn.sigmoid(v0 + token_shift(v_lo, mu_v) @ v2)

    def hl(t):
        return t.reshape(B, T, RW_HEADS, RW_HEAD)

    kk = hl(k * k_k).astype(f32)
    kk = kk / jnp.maximum(jnp.sqrt(jnp.sum(kk * kk, axis=-1, keepdims=True)), 1e-12)
    k = k * (1.0 + (a - 1.0) * k_a)

    def tm(t):
        return jnp.moveaxis(hl(t).astype(f32), 1, 0)

    y = wkv7_scan(tm(r), tm(log_w), tm(k), tm(v), jnp.moveaxis(kk, 1, 0), tm(a))
    y = jnp.moveaxis(y, 0, 1).reshape(B, T, RW_WIDTH).astype(feats.dtype)
    y = head_norm(y, RW_HEADS, ln_w, ln_b, eps=RW_LN_EPS, center=True)
    bonus = (jnp.sum(hl(r) * hl(k) * r_k, axis=-1, keepdims=True) * hl(v)).reshape(B, T, RW_WIDTH)
    return (y + bonus) * g, v_first


def gla_chunkwise(q, k, v, log_f):
    f32 = jnp.float32
    B, H, T, dk = q.shape
    dv = v.shape[-1]
    L = HG_CHUNK
    q, k, v, log_f = (t.astype(f32) for t in (q, k, v, log_f))
    causal = jnp.tril(jnp.ones((L, L), dtype=bool))

    def step(S, inp):
        qc, kc, vc, gc = inp
        b = jnp.cumsum(gc, axis=2)
        inter = jnp.einsum('bhtd,bhde->bhte', qc * jnp.exp(b), S)
        rel = jnp.minimum(b[:, :, :, None, :] - b[:, :, None, :, :], 0.0)
        decay = jnp.where(causal[:, :, None], jnp.exp(rel), 0.0)
        A = jnp.einsum('bhtd,bhtsd,bhsd->bhts', qc, decay, kc)
        o = inter + jnp.einsum('bhts,bhse->bhte', A, vc)
        b_end = b[:, :, -1:, :]
        S = (jnp.exp(b_end[:, :, 0, :])[..., None] * S
             + jnp.einsum('bhsd,bhse->bhde', kc * jnp.exp(b_end - b), vc))
        return S, o

    xs = (to_chunks(q, L), to_chunks(k, L), to_chunks(v, L), to_chunks(log_f, L))
    _, o = lax.scan(step, jnp.zeros((B, H, dk, dv), f32), xs)
    return from_chunks(o)


def hgrn2_branch(q_pre, f_pre, i_in, g_pre, lb, norm_w):
    fp = f_pre.astype(jnp.float32)
    q = jax.nn.silu(q_pre)
    sig = jax.nn.sigmoid(fp)
    k = (1.0 - lb) * (1.0 - sig)
    log_f = jnp.log(jnp.maximum(lb + (1.0 - lb) * sig, TINY))
    o = gla_chunkwise(split_heads(q, HG_HEADS), split_heads(k, HG_HEADS),
                      split_heads(i_in, HG_HEADS), split_heads(log_f, HG_HEADS))
    o = merge_heads(o).astype(q_pre.dtype)
    return head_norm(o, HG_HEADS, norm_w) * jax.nn.sigmoid(g_pre)


def setup_inputs(seed: int = 0) -> dict:
    key = jax.random.key(seed)
    ks = iter(jax.random.split(key, 48))

    def nrm(shape, s=1.0):
        return s * jax.random.normal(next(ks), shape, jnp.float32)

    L, D, Lv = DEPTH, D_MODEL, DEPTH - 1
    return {
        'x': nrm((BATCH, SEQ, D)),
        'c': nrm((BATCH, D)),
        'norm_w': 1.0 + nrm((L, N_SUB, D), 0.02),
        'final_norm_w': 1.0 + nrm((D,), 0.02),
        'ada_w': nrm((L, D, N_SUB * 3 * D), 0.1 * D ** -0.5),
        'ada_b': nrm((L, N_SUB * 3 * D), 0.02),
        'ffn_up': nrm((L, 2, D, 2 * D_FF), D ** -0.5),
        'ffn_down': nrm((L, 2, D_FF, D), D_FF ** -0.5),
        'w_in': nrm((L, D, N_IN), D ** -0.5),
        'w_in_vres': nrm((Lv, D, RW_MV_LORA), D ** -0.5),
        'ml_conv_w': nrm((L, ML_CONV, ML_WIDTH), ML_CONV ** -0.5),
        'ml_conv_b': nrm((L, ML_WIDTH), 0.02),
        'ml_wq': nrm((L, ML_HEADS, ML_HD, ML_HD), ML_HD ** -0.5),
        'ml_wk': nrm((L, ML_HEADS, ML_HD, ML_HD), ML_HD ** -0.5),
        'ml_i_b': nrm((L, ML_HEADS), 0.1),
        'ml_f_b': jnp.linspace(3.0, 6.0, ML_HEADS)[None] + nrm((L, ML_HEADS), 0.1),
        'ml_norm_w': 1.0 + nrm((L, ML_WIDTH), 0.02),
        'ml_skip': 1.0 + nrm((L, ML_WIDTH), 0.02),
        'rw_mu': jax.random.uniform(next(ks), (L, RW_SHIFT_COLS), jnp.float32),
        'rw_mu_vres': jax.random.uniform(next(ks), (Lv, RW_MV_LORA), jnp.float32),
        'rw_w0': jnp.linspace(-6.5, -1.5, RW_WIDTH)[None] + nrm((L, RW_WIDTH), 0.1),
        'rw_w2': nrm((L, RW_DECAY_LORA, RW_WIDTH), 0.1),
        'rw_a0': nrm((L, RW_WIDTH), 0.1),
        'rw_a2': nrm((L, RW_AAA_LORA, RW_WIDTH), 0.1),
        'rw_v0': 1.0 + nrm((Lv, RW_WIDTH), 0.1),
        'rw_v2': nrm((Lv, RW_MV_LORA, RW_WIDTH), 0.1),
        'rw_g2': nrm((L, RW_GATE_LORA, RW_WIDTH), RW_GATE_LORA ** -0.5),
        'rw_k_k': 0.85 + nrm((L, RW_WIDTH), 0.02),
        'rw_k_a': 1.0 + nrm((L, RW_WIDTH), 0.02),
        'rw_r_k': nrm((L, RW_HEADS, RW_HEAD), 0.1),
        'rw_ln_w': 1.0 + nrm((L, RW_WIDTH), 0.02),
        'rw_ln_b': nrm((L, RW_WIDTH), 0.02),
        'hg_lb_logits': nrm((L, HG_WIDTH), 0.5),
        'hg_norm_w': 1.0 + nrm((L, HG_WIDTH), 0.02),
        'w_branch': nrm((L, N_BRANCH, D, D), D ** -0.5),
        'w_out': nrm((L, D, D), D ** -0.5),
    }


def reference(x, c, norm_w, final_norm_w, ada_w, ada_b, ffn_up, ffn_down, w_in, w_in_vres,
              ml_conv_w, ml_conv_b, ml_wq, ml_wk, ml_i_b, ml_f_b, ml_norm_w, ml_skip,
              rw_mu, rw_mu_vres, rw_w0, rw_w2, rw_a0, rw_a2, rw_v0, rw_v2, rw_g2,
              rw_k_k, rw_k_a, rw_r_k, rw_ln_w, rw_ln_b, hg_lb_logits, hg_norm_w,
              w_branch, w_out):
    B, T, D = x.shape
    lb_soft = jax.nn.softmax(hg_lb_logits.astype(jnp.float32), axis=0)
    lower_bounds = jnp.cumsum(lb_soft, axis=0) - lb_soft[0]
    cond = jax.nn.silu(c)
    v_first = None
    for l in range(DEPTH):
        mod = (cond @ ada_w[l] + ada_b[l]).reshape(B, N_SUB, 3, 1, D)

        h = modulate(x, norm_w[l, 0], mod[:, 0, 0], mod[:, 0, 1])
        x = x + 0.5 * (1.0 + mod[:, 0, 2]) * swiglu(h, ffn_up[l, 0], ffn_down[l, 0])

        h = modulate(x, norm_w[l, 1], mod[:, 1, 0], mod[:, 1, 1])
        if l == 0:
            z = h @ w_in[l]
            parts = split_cols(z, IN_SPLITS)
            vres = None
        else:
            z = h @ jnp.concatenate([w_in[l], w_in_vres[l - 1]], axis=1)
            parts = split_cols(z, IN_SPLITS + (RW_MV_LORA,))
            vres = (parts[10], rw_mu_vres[l - 1], rw_v0[l - 1], rw_v2[l - 1])
        ml_x, ml_o, ml_i, ml_f, rw_feats, hg_q, hg_f, hg_i, hg_g, gate_pre = parts[:10]

        y_ml = mlstm_branch(ml_x, ml_o, ml_i, ml_f, ml_conv_w[l], ml_conv_b[l], ml_wq[l], ml_wk[l],
                            ml_i_b[l], ml_f_b[l], ml_norm_w[l], ml_skip[l])
        y_rw, v_first = rwkv7_branch(rw_feats, vres, v_first, rw_mu[l], rw_w0[l], rw_w2[l],
                                     rw_a0[l], rw_a2[l], rw_g2[l], rw_k_k[l], rw_k_a[l],
                                     rw_r_k[l], rw_ln_w[l], rw_ln_b[l])
        y_hg = hgrn2_branch(hg_q, hg_f, hg_i, hg_g, lower_bounds[l], hg_norm_w[l])

        y_br = jnp.einsum('btnw,nwd->btnd', jnp.stack([y_ml, y_rw, y_hg], axis=2), w_branch[l])
        gates = jax.nn.sigmoid(gate_pre.reshape(B, T, N_BRANCH, D))
        mixed = jnp.sum(gates * y_br, axis=2) @ w_out[l]
        x = x + (1.0 + mod[:, 1, 2]) * mixed

        h = modulate(x, norm_w[l, 2], mod[:, 2, 0], mod[:, 2, 1])
        x = x + 0.5 * (1.0 + mod[:, 2, 2]) * swiglu(h, ffn_up[l, 1], ffn_down[l, 1])
    return rms_norm(x, final_norm_w)
```

```python
import functools

import numpy as np
import jax
import jax.numpy as jnp
from jax import lax
from jax.experimental import pallas as pl
from jax.experimental.pallas import tpu as pltpu

F32 = jnp.float32
BF16 = jnp.bfloat16

EPS = 1e-6
NEG_BIG = -1e30
TINY = 1e-30

N_SUB = 3
N_BRANCH = 3
ML_HEADS = 4
ML_CONV = 4
RW_HEAD = 64
RW_LN_EPS = 64e-5
RW_DECAY_LORA = 64
RW_AAA_LORA = 64
RW_MV_LORA = 32
RW_GATE_LORA = 128
HG_EXPAND = 128

LANES = 128
VMEM_LIMIT = 56 * 1024 * 1024

FFN_TM = 512
FFN_FC = 256
INP_TM = 1024
INP_TN = 1408
MIX_TM = 512
ML_CHUNK = 256
RW_CHUNK = 64
HG_CHUNK = 128


def _cparams(sem):
    return pltpu.CompilerParams(dimension_semantics=sem, vmem_limit_bytes=VMEM_LIMIT)


def _dot(a, b):
    return jnp.dot(a, b, preferred_element_type=F32)


def _dot_t(a, b):
    return lax.dot_general(a, b, (((1,), (1,)), ((), ())), preferred_element_type=F32)


def _tdot(a, b):
    return lax.dot_general(a, b, (((0,), (0,)), ((), ())), preferred_element_type=F32)


def _dot_hi(a, b):
    return jnp.dot(a, b, preferred_element_type=F32, precision=lax.Precision.HIGHEST)


def _sigmoid(x):
    return 1.0 / (1.0 + jnp.exp(-x))


def _silu(x):
    return x * _sigmoid(x)


def _modulated_norm(x, nw, shift, scale):
    ms = jnp.mean(x * x, axis=-1, keepdims=True)
    return (x * lax.rsqrt(ms + EPS) * nw) * (1.0 + scale) + shift


def _ada_kernel(c_ref, w_ref, b_ref, o_ref):
    cond = _silu(c_ref[...])
    o_ref[0] = _dot(cond, w_ref[0]) + b_ref[0]


def _ada_mod(c, ada_w, ada_b):
    L, D, M = ada_w.shape
    B = c.shape[0]
    tn = M // 8
    return pl.pallas_call(
        _ada_kernel,
        out_shape=jax.ShapeDtypeStruct((L, B, M), F32),
        grid=(L, M // tn),
        in_specs=[
            pl.BlockSpec((B, D), lambda l, j: (0, 0)),
            pl.BlockSpec((1, D, tn), lambda l, j: (l, 0, j)),
            pl.BlockSpec((1, 1, tn), lambda l, j: (l, 0, j)),
        ],
        out_specs=pl.BlockSpec((1, B, tn), lambda l, j: (l, 0, j)),
        compiler_params=_cparams(("arbitrary", "arbitrary")),
        name="ada_mod",
    )(c, ada_w, ada_b.reshape(L, 1, M))


def _ffn_kernel(x_ref, mod_ref, nw_ref, wa_ref, wb_ref, wd_ref, fw_ref, o_ref, acc_ref,
                *, sub, final_norm):
    x = x_ref[...]
    mod = mod_ref[0]
    shift, scale, gate = (mod[3 * sub + i:3 * sub + i + 1] for i in range(3))
    hb = _modulated_norm(x, nw_ref[...], shift, scale).astype(BF16)
    acc_ref[...] = jnp.zeros_like(acc_ref)

    def step(c, carry):
        a = _dot(hb, wa_ref[c])
        b = _dot(hb, wb_ref[c])
        act = (_silu(a) * b).astype(BF16)
        acc_ref[...] += _dot(act, wd_ref[c])
        return carry

    lax.fori_loop(0, wa_ref.shape[0], step, 0)
    y = x + (0.5 * (1.0 + gate)) * acc_ref[...]
    if final_norm:
        ms = jnp.mean(y * y, axis=-1, keepdims=True)
        y = y * lax.rsqrt(ms + EPS) * fw_ref[...]
    o_ref[...] = y


def _ffn(x, mod, nw, w_up, w_down, fw, *, sub, tokens_per_batch, final_norm):
    N, D = x.shape
    F = w_down.shape[0]
    nc = F // FFN_FC
    wa = w_up[:, :F].astype(BF16).reshape(D, nc, FFN_FC).transpose(1, 0, 2)
    wb = w_up[:, F:].astype(BF16).reshape(D, nc, FFN_FC).transpose(1, 0, 2)
    wd = w_down.astype(BF16).reshape(nc, FFN_FC, D)
    bpb = tokens_per_batch // FFN_TM
    const3 = lambda i: (0, 0, 0)
    return pl.pallas_call(
        functools.partial(_ffn_kernel, sub=sub, final_norm=final_norm),
        out_shape=jax.ShapeDtypeStruct((N, D), F32),
        grid=(N // FFN_TM,),
        in_specs=[
            pl.BlockSpec((FFN_TM, D), lambda i: (i, 0)),
            pl.BlockSpec((1,) + mod.shape[1:], lambda i: (i // bpb, 0, 0)),
            pl.BlockSpec((1, D), lambda i: (0, 0)),
            pl.BlockSpec(wa.shape, const3, pipeline_mode=pl.Buffered(1)),
            pl.BlockSpec(wb.shape, const3, pipeline_mode=pl.Buffered(1)),
            pl.BlockSpec(wd.shape, const3, pipeline_mode=pl.Buffered(1)),
            pl.BlockSpec((1, D), lambda i: (0, 0)),
        ],
        out_specs=pl.BlockSpec((FFN_TM, D), lambda i: (i, 0)),
        scratch_shapes=[pltpu.VMEM((FFN_TM, D), F32)],
        compiler_params=_cparams(("arbitrary",)),
        name="ffn_half",
    )(x, mod, nw.reshape(1, D), wa, wb, wd, fw.reshape(1, D))


def _inproj_kernel(x_ref, mod_ref, nw_ref, w_ref, o_ref, h_ref, *, sub):
    @pl.when(pl.program_id(1) == 0)
    def _():
        mod = mod_ref[0]
        shift, scale = mod[3 * sub:3 * sub + 1], mod[3 * sub + 1:3 * sub + 2]
        h_ref[...] = _modulated_norm(x_ref[...], nw_ref[...], shift, scale).astype(BF16)

    o_ref[...] = _dot(h_ref[...], w_ref[...])


def _inproj(x, mod, nw, w, *, sub, tokens_per_batch):
    N, D = x.shape
    M = w.shape[1]
    tm = min(INP_TM, tokens_per_batch)
    bpb = tokens_per_batch // tm
    return pl.pallas_call(
        functools.partial(_inproj_kernel, sub=sub),
        out_shape=jax.ShapeDtypeStruct((N, M), F32),
        grid=(N // tm, M // INP_TN),
        in_specs=[
            pl.BlockSpec((tm, D), lambda i, j: (i, 0)),
            pl.BlockSpec((1,) + mod.shape[1:], lambda i, j: (i // bpb, 0, 0)),
            pl.BlockSpec((1, D), lambda i, j: (0, 0)),
            pl.BlockSpec((D, INP_TN), lambda i, j: (0, j)),
        ],
        out_specs=pl.BlockSpec((tm, INP_TN), lambda i, j: (i, j)),
        scratch_shapes=[pltpu.VMEM((tm, D), BF16)],
        compiler_params=_cparams(("arbitrary", "arbitrary")),
        name="mixer_inproj",
    )(x, mod, nw.reshape(1, D), w)


def _mix_kernel(x_ref, mod_ref, yml_ref, yrw_ref, yhg_ref, g0_ref, g1_ref, g2_ref,
                wbr_ref, wout_ref, o_ref, *, sub):
    gate = mod_ref[0][3 * sub + 2:3 * sub + 3]
    mixed = None
    for n, (y_ref, g_ref) in enumerate(((yml_ref, g0_ref), (yrw_ref, g1_ref), (yhg_ref, g2_ref))):
        ybr = _dot(y_ref[...].astype(BF16), wbr_ref[n])
        term = _sigmoid(g_ref[...]) * ybr
        mixed = term if mixed is None else mixed + term
    o_ref[...] = x_ref[...] + (1.0 + gate) * _dot(mixed.astype(BF16), wout_ref[...])


def _branch_mix(x, mod, y_ml, y_rw, y_hg, z, gate_col0, w_branch, w_out, *, sub, tokens_per_batch):
    N, D = x.shape
    bpb = tokens_per_batch // MIX_TM
    g0 = gate_col0 // D
    tok = lambda i: (i, 0)
    return pl.pallas_call(
        functools.partial(_mix_kernel, sub=sub),
        out_shape=jax.ShapeDtypeStruct((N, D), F32),
        grid=(N // MIX_TM,),
        in_specs=[
            pl.BlockSpec((MIX_TM, D), tok),
            pl.BlockSpec((1,) + mod.shape[1:], lambda i: (i // bpb, 0, 0)),
            pl.BlockSpec((MIX_TM, D), tok),
            pl.BlockSpec((MIX_TM, D), tok),
            pl.BlockSpec((MIX_TM, D), tok),
            pl.BlockSpec((MIX_TM, D), lambda i: (i, g0)),
            pl.BlockSpec((MIX_TM, D), lambda i: (i, g0 + 1)),
            pl.BlockSpec((MIX_TM, D), lambda i: (i, g0 + 2)),
            pl.BlockSpec(w_branch.shape, lambda i: (0, 0, 0), pipeline_mode=pl.Buffered(1)),
            pl.BlockSpec(w_out.shape, lambda i: (0, 0), pipeline_mode=pl.Buffered(1)),
        ],
        out_specs=pl.BlockSpec((MIX_TM, D), tok),
        compiler_params=_cparams(("arbitrary",)),
        name="branch_mix",
    )(x, mod, y_ml, y_rw, y_hg, z, z, z, w_branch.astype(BF16), w_out.astype(BF16))


def _log_sigmoid(x):
    return jnp.minimum(x, 0.0) - jnp.log1p(jnp.exp(-jnp.abs(x)))


def _mlstm_kernel(bias_ref, x_ref, op_ref, ig_ref, fg_ref, cw_ref, cb_ref, wq_ref, wk_ref,
                  nw_ref, sk_ref, y_ref, c_ref, n_ref, m_ref, tail_ref):
    head = pl.program_id(1)
    L, dh = x_ref.shape[1], x_ref.shape[2]

    @pl.when(pl.program_id(2) == 0)
    def _():
        c_ref[...] = jnp.zeros_like(c_ref)
        n_ref[...] = jnp.zeros_like(n_ref)
        m_ref[...] = jnp.zeros_like(m_ref)
        tail_ref[...] = jnp.zeros_like(tail_ref)

    x = x_ref[0]
    cw = cw_ref[...]
    tail = tail_ref[...]
    row8 = lax.broadcasted_iota(jnp.int32, (8, dh), 0)
    conv = x * cw[ML_CONV - 1:ML_CONV]
    for s in range(1, ML_CONV):
        xr = pltpu.roll(x, s, 0)
        top = jnp.where(row8 < s, pltpu.roll(tail, s, 0), xr[0:8])
        xs = jnp.concatenate([top, xr[8:]], axis=0)
        conv = conv + xs * cw[ML_CONV - 1 - s:ML_CONV - s]
    tail_ref[...] = x[L - 8:L]
    xc = _silu(conv + cb_ref[...])
    xcb = xc.astype(BF16)
    q = _dot(xcb, wq_ref[0])
    k = _dot(xcb, wk_ref[0]) * (dh ** -0.5)
    qb, kb, vb = q.astype(BF16), k.astype(BF16), x.astype(BF16)

    li = ig_ref[0, 0] + bias_ref[head]
    lf = _log_sigmoid(fg_ref[0, 0] + bias_ref[ML_HEADS + head])
    r = lax.broadcasted_iota(jnp.int32, (L, L), 0)
    c = lax.broadcasted_iota(jnp.int32, (L, L), 1)
    b_row = _dot_hi(lf, (r <= c).astype(F32))
    cols = jnp.concatenate([b_row, li, jnp.zeros((LANES - 2, L), F32)], axis=0).T
    b_col, li_col = cols[:, 0:1], cols[:, 1:2]

    m_prev = m_ref[...][:, 0:1]
    log_d = jnp.where(r >= c, b_col - b_row + li, NEG_BIG)
    log_inter = b_col + m_prev
    m_t = jnp.maximum(log_inter, jnp.max(log_d, axis=-1, keepdims=True))
    w_intra = jnp.exp(log_d - m_t)
    w_inter = jnp.exp(log_inter - m_t)
    s_mat = _dot_t(qb, kb) * w_intra
    n_row = n_ref[...]
    num = w_inter * _dot(qb, c_ref[...].astype(BF16)) + _dot(s_mat.astype(BF16), vb)
    den = (w_inter * jnp.sum(q * n_row, axis=-1, keepdims=True)
           + jnp.sum(s_mat, axis=-1, keepdims=True))
    h = num / jnp.maximum(jnp.abs(den), jnp.exp(-m_t))

    b_end = b_row[:, L - 1:L]
    log_ws = b_end - b_col + li_col
    m_new = jnp.maximum(b_end + m_prev, jnp.max(log_ws, axis=0, keepdims=True))
    decay = jnp.exp(b_end + m_prev - m_new)
    kw = k * jnp.exp(log_ws - m_new)
    c_ref[...] = decay * c_ref[...] + _tdot(kw.astype(BF16), vb)
    n_ref[...] = decay * n_row + jnp.sum(kw, axis=0, keepdims=True)
    m_ref[...] = jnp.broadcast_to(m_new, m_ref.shape)

    hg = h * _sigmoid(op_ref[0])
    hg = hg - jnp.mean(hg, axis=-1, keepdims=True)
    hn = hg * lax.rsqrt(jnp.mean(hg * hg, axis=-1, keepdims=True) + EPS)
    y_ref[0] = hn * nw_ref[...] + sk_ref[...] * xc


def _mlstm(z3, gates, bias, conv_w, conv_b, wq, wk, norm_w, skip, *, width):
    B, T, _ = z3.shape
    dh = width // ML_HEADS
    L = min(ML_CHUNK, T)
    o_blk = width // dh
    vec = lambda b, h, t: (0, h)
    return pl.pallas_call(
        _mlstm_kernel,
        out_shape=jax.ShapeDtypeStruct((B, T, width), F32),
        grid=(B, ML_HEADS, T // L),
        in_specs=[
            pl.BlockSpec(memory_space=pltpu.SMEM),
            pl.BlockSpec((1, L, dh), lambda b, h, t: (b, t, h)),
            pl.BlockSpec((1, L, dh), lambda b, h, t: (b, t, o_blk + h)),
            pl.BlockSpec((1, 1, 1, L), lambda b, h, t: (b, h, 0, t)),
            pl.BlockSpec((1, 1, 1, L), lambda b, h, t: (b, ML_HEADS + h, 0, t)),
            pl.BlockSpec((ML_CONV, dh), vec),
            pl.BlockSpec((1, dh), vec),
            pl.BlockSpec((1, dh, dh), lambda b, h, t: (h, 0, 0)),
            pl.BlockSpec((1, dh, dh), lambda b, h, t: (h, 0, 0)),
            pl.BlockSpec((1, dh), vec),
            pl.BlockSpec((1, dh), vec),
        ],
        out_specs=pl.BlockSpec((1, L, dh), lambda b, h, t: (b, t, h)),
        scratch_shapes=[
            pltpu.VMEM((dh, dh), F32),
            pltpu.VMEM((1, dh), F32),
            pltpu.VMEM((1, LANES), F32),
            pltpu.VMEM((8, dh), F32),
        ],
        compiler_params=_cparams(("arbitrary", "arbitrary", "arbitrary")),
        name="mlstm",
    )(bias, z3, z3, gates, gates, conv_w, conv_b.reshape(1, width), wq.astype(BF16),
      wk.astype(BF16), norm_w.reshape(1, width), skip.reshape(1, width))


def _midrow_broadcast(b, m):
    n, w = b.shape
    row = lambda i: b[i:i + 1]
    if m >= 8:
        return jnp.concatenate(
            [jnp.broadcast_to(row(p * 2 * m + m), (2 * m, w)) for p in range(n // (2 * m))], axis=0)
    pick = lambda off: jnp.concatenate(
        [jnp.broadcast_to(row(8 * g + off), (8, w)) for g in range(n // 8)], axis=0)
    sub = lax.broadcasted_iota(jnp.int32, (n, w), 0) % 8
    if m == 4:
        return pick(4)
    if m == 2:
        return jnp.where(sub < 4, pick(2), pick(6))
    return jnp.where(sub < 2, pick(1), jnp.where(sub < 4, pick(3), jnp.where(sub < 6, pick(5), pick(7))))


def _hgrn_kernel(q_ref, f_ref, i_ref, g_ref, lb_ref, nw_ref, y_ref, s_ref):
    C, W = q_ref.shape[1], q_ref.shape[2]
    H = W // HG_EXPAND

    @pl.when(pl.program_id(1) == 0)
    def _():
        s_ref[...] = jnp.zeros_like(s_ref)

    lb = lb_ref[...]
    sig = _sigmoid(f_ref[0])
    q = _silu(q_ref[0])
    k = (1.0 - lb) * (1.0 - sig)
    log_f = jnp.log(jnp.maximum(lb + (1.0 - lb) * sig, TINY))
    r = lax.broadcasted_iota(jnp.int32, (C, C), 0)
    c = lax.broadcasted_iota(jnp.int32, (C, C), 1)
    b = _dot_hi((r >= c).astype(F32), log_f)
    v = i_ref[0]
    vb = v.astype(BF16)

    rowi = lax.broadcasted_iota(jnp.int32, (C, W), 0)
    xs, ys, masks = [], [], []
    m = 1
    while m < C:
        e = jnp.exp(-jnp.abs(b - _midrow_broadcast(b, m)))
        second = (rowi % (2 * m)) >= m
        xs.append(jnp.where(second, q * e, 0.0).astype(BF16))
        ys.append(jnp.where(second, 0.0, k * e).astype(BF16))
        masks.append((r // (2 * m)) == (c // (2 * m)))
        m *= 2

    qe = (q * jnp.exp(b)).astype(BF16)
    b_end = b[C - 1:C]
    kd = (k * jnp.exp(b_end - b)).astype(BF16)
    g_end = jnp.exp(b_end)
    diag = q * k
    gate = _sigmoid(g_ref[0])
    nw = nw_ref[...]
    for h in range(H):
        sl = slice(h * HG_EXPAND, (h + 1) * HG_EXPAND)
        st = s_ref[h]
        a = jnp.zeros((C, C), F32)
        for x_l, y_l, mask in zip(xs, ys, masks):
            a = a + jnp.where(mask, _dot_t(x_l[:, sl], y_l[:, sl]), 0.0)
        o = (_dot_t(qe[:, sl], st.astype(BF16)) + _dot(a.astype(BF16), vb[:, sl])
             + jnp.sum(diag[:, sl], axis=-1, keepdims=True) * v[:, sl])
        s_ref[h] = st * g_end[:, sl] + _tdot(vb[:, sl], kd[:, sl])
        on = o * lax.rsqrt(jnp.mean(o * o, axis=-1, keepdims=True) + EPS)
        y_ref[0, :, sl] = on * nw[:, sl] * gate[:, sl]


def _hgrn(z3, lower_bound, norm_w, *, col0, width):
    B, T, _ = z3.shape
    C = min(HG_CHUNK, T)
    H = width // HG_EXPAND
    blk0 = col0 // width
    spec = lambda j: pl.BlockSpec((1, C, width), lambda b, t: (b, t, blk0 + j))
    vec = pl.BlockSpec((1, width), lambda b, t: (0, 0))
    return pl.pallas_call(
        _hgrn_kernel,
        out_shape=jax.ShapeDtypeStruct((B, T, width), F32),
        grid=(B, T // C),
        in_specs=[spec(0), spec(1), spec(2), spec(3), vec, vec],
        out_specs=pl.BlockSpec((1, C, width), lambda b, t: (b, t, 0)),
        scratch_shapes=[pltpu.VMEM((H, HG_EXPAND, HG_EXPAND), F32)],
        compiler_params=_cparams(("arbitrary", "arbitrary")),
        name="hgrn2",
    )(z3, z3, z3, z3, lower_bound.reshape(1, width), norm_w.reshape(1, width))


def _token_shift(z, mu, last_ref):
    n = z.shape[0]
    row = lax.broadcasted_iota(jnp.int32, z.shape, 0)
    prev = jnp.where(row == 0, last_ref[7:8], pltpu.roll(z, 1, 0))
    last_ref[...] = z[n - 8:n]
    return z + mu * (prev - z)


def _lane_groups_to_rows(x):
    return jnp.concatenate([x[:, g * LANES:(g + 1) * LANES] for g in range(x.shape[1] // LANES)], axis=0)


def _head_sum(x, ones_bd):
    n, w = x.shape
    xs = _lane_groups_to_rows(x)
    hi = xs.astype(BF16)
    lo = (xs - hi.astype(F32)).astype(BF16)
    s = _dot(hi, ones_bd) + _dot(lo, ones_bd)
    return jnp.concatenate([s[g * n:(g + 1) * n] for g in range(w // LANES)], axis=1)


def _softplus(x):
    return jnp.maximum(x, 0.0) + jnp.log1p(jnp.exp(-jnp.abs(x)))


def _rwkv_kernel(*refs, has_vres):
    it = iter(refs)
    r_ref, k_ref, v_ref, lo_ref = next(it), next(it), next(it), next(it)
    sm_ref, vf_ref = (next(it), next(it)) if has_vres else (None, None)
    mu_ref, mulo_ref, vec_ref, w2_ref, a2_ref, g2_ref = (next(it) for _ in range(6))
    musm_ref, v2_ref = (next(it), next(it)) if has_vres else (None, None)
    y_ref = next(it)
    vfo_ref = None if has_vres else next(it)
    s_ref, pr_ref, pk_ref, pv_ref, plo_ref = (next(it) for _ in range(5))
    psm_ref = next(it) if has_vres else None

    C, W = r_ref.shape[1], r_ref.shape[2]
    P = W // LANES

    @pl.when(pl.program_id(1) == 0)
    def _():
        s_ref[...] = jnp.zeros_like(s_ref)
        for ref in (pr_ref, pk_ref, pv_ref, plo_ref, psm_ref):
            if ref is not None:
                ref[...] = jnp.zeros_like(ref)

    vec = vec_ref[...]
    w0, a0, k_k, k_a, ln_w, ln_b, r_k, v0 = (vec[i:i + 1] for i in range(8))
    mu = mu_ref[...]
    r = _token_shift(r_ref[0], mu[0:1], pr_ref)
    k = _token_shift(k_ref[0], mu[1:2], pk_ref)
    v = _token_shift(v_ref[0], mu[2:3], pv_ref)
    lo = _token_shift(lo_ref[0], mulo_ref[...], plo_ref)
    lo_wa, lo_g = lo[:, :LANES], lo[:, LANES:]

    w_raw = -_softplus(-(w0 + _dot(jnp.tanh(lo_wa).astype(BF16), w2_ref[...]))) - 0.5
    lw = -jnp.exp(w_raw)
    a = _sigmoid(a0 + _dot(lo_wa.astype(BF16), a2_ref[...]))
    g = _dot(_sigmoid(lo_g).astype(BF16), g2_ref[...])
    if has_vres:
        sm = _token_shift(sm_ref[0], musm_ref[...], psm_ref)
        v = v + (vf_ref[0] - v) * _sigmoid(v0 + _dot(sm.astype(BF16), v2_ref[...]))
    else:
        vfo_ref[0] = v

    rr = lax.broadcasted_iota(jnp.int32, (LANES, LANES), 0)
    cc = lax.broadcasted_iota(jnp.int32, (LANES, LANES), 1)
    ones_bd = ((rr // RW_HEAD) == (cc // RW_HEAD)).astype(BF16)

    kk = k * k_k
    kk = kk / jnp.maximum(jnp.sqrt(_head_sum(kk * kk, ones_bd)), 1e-12)
    kmod = k * (1.0 + (a - 1.0) * k_a)
    ahat = kk * a

    tr = lax.broadcasted_iota(jnp.int32, (C, C), 0)
    tc = lax.broadcasted_iota(jnp.int32, (C, C), 1)
    cum = _dot_hi((tr >= tc).astype(F32), lw)
    e_inc = jnp.exp(cum)
    e_inv = jnp.exp(-cum)
    r_dec = r * e_inc
    kk_dec = kk * jnp.exp(cum - lw)
    a_inv = ahat * e_inv
    k_inv = kmod * e_inv
    gamma_end = e_inc[C - 1:C]

    lane = lax.broadcasted_iota(jnp.int32, (C, LANES), 1)
    first = lane < RW_HEAD

    def stack(zp):
        return jnp.concatenate([jnp.where(first, zp, 0.0), jnp.where(first, 0.0, zp)], axis=0).astype(BF16)

    strict = rr > cc
    incl = rr >= cc
    eye = (rr == cc).astype(F32)
    n_sq = int(np.log2(C)) - 1
    ys = []
    for p in range(P):
        sl = slice(p * LANES, (p + 1) * LANES)
        kks, rs, ais, kis, vs = (stack(t[:, sl]) for t in (kk_dec, r_dec, a_inv, k_inv, v))
        ak = jnp.concatenate([ais, kis], axis=0)
        sc = _dot_t(jnp.concatenate([kks, rs], axis=0), ak)
        n2 = 2 * C
        l_a = jnp.where(strict, sc[:n2, :n2], 0.0)
        l_k = jnp.where(strict, sc[:n2, n2:], 0.0)
        a_ra = jnp.where(incl, sc[n2:, :n2], 0.0)
        a_rk = jnp.where(incl, sc[n2:, n2:], 0.0)
        t_inv = eye - l_a
        pw = l_a.astype(BF16)
        for _ in range(n_sq):
            pw32 = _dot(pw, pw)
            pw = pw32.astype(BF16)
            t_inv = t_inv + _dot(t_inv.astype(BF16), pw)
        lkv = _dot(l_k.astype(BF16), vs)
        rhs = jnp.concatenate([-kks.astype(F32), -lkv], axis=1).astype(BF16)
        wv = _dot(t_inv.astype(BF16), rhs)
        st = s_ref[p]
        wr = _dot_t(jnp.concatenate([wv[:, :LANES].astype(BF16), rs], axis=0), st.astype(BF16))
        u = wr[:n2] + wv[:, LANES:]
        uv = jnp.concatenate([u.astype(BF16), vs], axis=0)
        yst = wr[n2:] + _dot(jnp.concatenate([a_ra, a_rk], axis=1).astype(BF16), uv)
        ys.append(yst[:C] + yst[C:])
        s_ref[p] = (st + _tdot(uv, ak)) * gamma_end[:, sl]
    y = jnp.concatenate(ys, axis=1)

    inv_n = 1.0 / RW_HEAD
    yc = y - _head_sum(y, ones_bd) * inv_n
    yn = yc * lax.rsqrt(_head_sum(yc * yc, ones_bd) * inv_n + RW_LN_EPS) * ln_w + ln_b
    bonus = _head_sum(r * kmod * r_k, ones_bd) * v
    y_ref[0] = (yn + bonus) * g


def _rwkv(z3, v_first, prm, *, rkv_col0, lora_col0, small_col0, width):
    B, T, _ = z3.shape
    C = min(RW_CHUNK, T)
    has_vres = v_first is not None
    lora_w = RW_DECAY_LORA + RW_AAA_LORA + RW_GATE_LORA
    tok = lambda wdt, col0, j=0: pl.BlockSpec((1, C, wdt), lambda b, t: (b, t, col0 // wdt + j))
    full = lambda arr: pl.BlockSpec(arr.shape, lambda b, t: (0,) * arr.ndim)
    tok_out = pl.BlockSpec((1, C, width), lambda b, t: (b, t, 0))

    args = [z3, z3, z3, z3]
    specs = [tok(width, rkv_col0, 0), tok(width, rkv_col0, 1), tok(width, rkv_col0, 2), tok(lora_w, lora_col0)]
    if has_vres:
        args += [z3, v_first]
        specs += [tok(LANES, small_col0), tok_out]
    weights = [prm["mu_rkv"], prm["mu_lora"], prm["vecs"], prm["w2"], prm["a2"], prm["g2"]]
    if has_vres:
        weights += [prm["mu_small"], prm["v2"]]
    args += weights
    specs += [full(wt) for wt in weights]

    y_shape = jax.ShapeDtypeStruct((B, T, width), F32)
    scratch = [pltpu.VMEM((width // LANES, LANES, LANES), F32)]
    scratch += [pltpu.VMEM((8, width), F32)] * 3 + [pltpu.VMEM((8, lora_w), F32)]
    if has_vres:
        scratch += [pltpu.VMEM((8, LANES), F32)]
    out = pl.pallas_call(
        functools.partial(_rwkv_kernel, has_vres=has_vres),
        out_shape=y_shape if has_vres else (y_shape, y_shape),
        grid=(B, T // C),
        in_specs=specs,
        out_specs=tok_out if has_vres else (tok_out, tok_out),
        scratch_shapes=scratch,
        compiler_params=_cparams(("arbitrary", "arbitrary")),
        name="rwkv7",
    )(*args)
    return (out, v_first) if has_vres else out


def _pad_rows(w, row0, rows):
    return jnp.zeros((rows, w.shape[1]), w.dtype).at[row0:row0 + w.shape[0]].set(w)


def kernel(x, c, norm_w, final_norm_w, ada_w, ada_b, ffn_up, ffn_down, w_in, w_in_vres,
           ml_conv_w, ml_conv_b, ml_wq, ml_wk, ml_i_b, ml_f_b, ml_norm_w, ml_skip,
           rw_mu, rw_mu_vres, rw_w0, rw_w2, rw_a0, rw_a2, rw_v0, rw_v2, rw_g2,
           rw_k_k, rw_k_a, rw_r_k, rw_ln_w, rw_ln_b, hg_lb_logits, hg_norm_w,
           w_branch, w_out):
    B, T, D = x.shape
    depth = norm_w.shape[0]
    N = B * T
    W = D
    lora_w = RW_DECAY_LORA + RW_AAA_LORA + RW_GATE_LORA

    o_mlx, o_mlo, o_mli, o_mlf = 0, W, 2 * W, 2 * W + ML_HEADS
    o_rw = 2 * W + 2 * ML_HEADS
    o_hg = o_rw + 3 * W + lora_w
    o_gate = o_hg + 4 * W
    n_in = o_gate + N_BRANCH * D
    c_rkv, c_hg, c_gate = 2 * W, 5 * W, 9 * W
    c_lora = c_gate + N_BRANCH * D
    c_small = c_lora + lora_w
    n_cols = c_small + LANES
    s_vlo = 2 * ML_HEADS

    lb_soft = jax.nn.softmax(hg_lb_logits.astype(F32), axis=0)
    lower_bounds = jnp.cumsum(lb_soft, axis=0) - lb_soft[0]

    mod_all = _ada_mod(c, ada_w, ada_b).reshape(depth, B, N_SUB * 3, D)
    x2 = x.reshape(N, D)
    v_first = None
    for l in range(depth):
        mod = mod_all[l]
        wl = w_in[l]
        small = jnp.concatenate([wl[:, o_mli:o_mli + 2 * ML_HEADS]]
                                + ([w_in_vres[l - 1]] if l > 0 else []), axis=1)
        small = jnp.pad(small, ((0, 0), (0, LANES - small.shape[1])))
        w_cat = jnp.concatenate([
            wl[:, o_mlx:o_mlx + 2 * W], wl[:, o_rw:o_rw + 3 * W], wl[:, o_hg:o_hg + 4 * W],
            wl[:, o_gate:n_in], wl[:, o_rw + 3 * W:o_rw + 3 * W + lora_w], small], axis=1).astype(BF16)

        x2 = _ffn(x2, mod, norm_w[l, 0], ffn_up[l, 0], ffn_down[l, 0], final_norm_w,
                  sub=0, tokens_per_batch=T, final_norm=False)

        z = _inproj(x2, mod, norm_w[l, 1], w_cat, sub=1, tokens_per_batch=T)
        z3 = z.reshape(B, T, n_cols)

        gates = z3[:, :, c_small:c_small + 2 * ML_HEADS].transpose(0, 2, 1).reshape(B, 2 * ML_HEADS, 1, T)
        y_ml = _mlstm(z3, gates, jnp.concatenate([ml_i_b[l], ml_f_b[l]]), ml_conv_w[l], ml_conv_b[l],
                      ml_wq[l], ml_wk[l], ml_norm_w[l], ml_skip[l], width=W)

        mu = rw_mu[l]
        prm = {
            "mu_rkv": mu[:3 * W].reshape(3, W),
            "mu_lora": mu[3 * W:].reshape(1, lora_w),
            "vecs": jnp.stack([rw_w0[l], rw_a0[l], rw_k_k[l], rw_k_a[l], rw_ln_w[l], rw_ln_b[l],
                               rw_r_k[l].reshape(W), rw_v0[l - 1] if l > 0 else jnp.zeros((W,), F32)]),
            "w2": _pad_rows(rw_w2[l], 0, LANES).astype(BF16),
            "a2": _pad_rows(rw_a2[l], RW_DECAY_LORA, LANES).astype(BF16),
            "g2": rw_g2[l].astype(BF16),
        }
        if l > 0:
            prm["mu_small"] = jnp.zeros((1, LANES), F32).at[0, s_vlo:s_vlo + RW_MV_LORA].set(rw_mu_vres[l - 1])
            prm["v2"] = _pad_rows(rw_v2[l - 1], s_vlo, LANES).astype(BF16)
        y_rw, v_first = _rwkv(z3, v_first, prm, rkv_col0=c_rkv, lora_col0=c_lora,
                              small_col0=c_small, width=W)

        y_hg = _hgrn(z3, lower_bounds[l], hg_norm_w[l], col0=c_hg, width=W)

        x2 = _branch_mix(x2, mod, y_ml.reshape(N, W), y_rw.reshape(N, W), y_hg.reshape(N, W), z,
                         c_gate, w_branch[l], w_out[l], sub=1, tokens_per_batch=T)

        x2 = _ffn(x2, mod, norm_w[l, 2], ffn_up[l, 1], ffn_down[l, 1], final_norm_w,
                  sub=2, tokens_per_batch=T, final_norm=(l == depth - 1))
    return x2.reshape(B, T, D)
```

```python
import functools

import numpy as np
import jax
import jax.numpy as jnp
from jax import lax
from jax.experimental import pallas as pl
from jax.experimental.pallas import tpu as pltpu

F32 = jnp.float32
BF16 = jnp.bfloat16

EPS = 1e-6
NEG_BIG = -1e30
TINY = 1e-30

N_SUB = 3
N_BRANCH = 3
ML_HEADS = 4
ML_CONV = 4
RW_HEAD = 64
RW_LN_EPS = 64e-5
RW_DECAY_LORA = 64
RW_AAA_LORA = 64
RW_MV_LORA = 32
RW_GATE_LORA = 128
HG_EXPAND = 128

LANES = 128
VMEM_LIMIT = 56 * 1024 * 1024

FFN_TM = 512
FFN_FC = 256
INP_TM = 1024
INP_TN = 1408
MIX_TM = 512
ML_CHUNK = 256
RW_CHUNK = 64
HG_CHUNK = 128


def _cparams(sem):
    return pltpu.CompilerParams(dimension_semantics=sem, vmem_limit_bytes=VMEM_LIMIT)


def _dot(a, b):
    return jnp.dot(a, b, preferred_element_type=F32)


def _dot_t(a, b):
    return lax.dot_general(a, b, (((1,), (1,)), ((), ())), preferred_element_type=F32)


def _tdot(a, b):
    return lax.dot_general(a, b, (((0,), (0,)), ((), ())), preferred_element_type=F32)


def _dot_hi(a, b):
    return jnp.dot(a, b, preferred_element_type=F32, precision=lax.Precision.HIGHEST)


def _sigmoid(x):
    return 1.0 / (1.0 + jnp.exp(-x))


def _silu(x):
    return x * _sigmoid(x)


def _modulated_norm(x, nw, shift, scale):
    ms = jnp.mean(x * x, axis=-1, keepdims=True)
    return (x * lax.rsqrt(ms + EPS) * nw) * (1.0 + scale) + shift


def _ada_kernel(c_ref, w_ref, b_ref, o_ref):
    cond = _silu(c_ref[...])
    o_ref[0] = _dot(cond, w_ref[0]) + b_ref[0]


def _ada_mod(c, ada_w, ada_b):
    L, D, M = ada_w.shape
    B = c.shape[0]
    tn = M // 8
    return pl.pallas_call(
        _ada_kernel,
        out_shape=jax.ShapeDtypeStruct((L, B, M), F32),
        grid=(L, M // tn),
        in_specs=[
            pl.BlockSpec((B, D), lambda l, j: (0, 0)),
            pl.BlockSpec((1, D, tn), lambda l, j: (l, 0, j)),
            pl.BlockSpec((1, 1, tn), lambda l, j: (l, 0, j)),
        ],
        out_specs=pl.BlockSpec((1, B, tn), lambda l, j: (l, 0, j)),
        compiler_params=_cparams(("arbitrary", "arbitrary")),
        name="ada_mod",
    )(c, ada_w, ada_b.reshape(L, 1, M))


def _ffn_kernel(x_ref, mod_ref, nw_ref, wa_ref, wb_ref, wd_ref, fw_ref, o_ref, acc_ref,
                *, sub, final_norm):
    x = x_ref[...]
    mod = mod_ref[0]
    shift, scale, gate = (mod[3 * sub + i:3 * sub + i + 1] for i in range(3))
    hb = _modulated_norm(x, nw_ref[...], shift, scale).astype(BF16)
    acc_ref[...] = jnp.zeros_like(acc_ref)

    def step(c, carry):
        a = _dot(hb, wa_ref[c])
        b = _dot(hb, wb_ref[c])
        act = (_silu(a) * b).astype(BF16)
        acc_ref[...] += _dot(act, wd_ref[c])
        return carry

    lax.fori_loop(0, wa_ref.shape[0], step, 0)
    y = x + (0.5 * (1.0 + gate)) * acc_ref[...]
    if final_norm:
        ms = jnp.mean(y * y, axis=-1, keepdims=True)
        y = y * lax.rsqrt(ms + EPS) * fw_ref[...]
    o_ref[...] = y


def _ffn(x, mod, nw, w_up, w_down, fw, *, sub, tokens_per_batch, final_norm):
    N, D = x.shape
    F = w_down.shape[0]
    nc = F // FFN_FC
    wa = w_up[:, :F].astype(BF16).reshape(D, nc, FFN_FC).transpose(1, 0, 2)
    wb = w_up[:, F:].astype(BF16).reshape(D, nc, FFN_FC).transpose(1, 0, 2)
    wd = w_down.astype(BF16).reshape(nc, FFN_FC, D)
    bpb = tokens_per_batch // FFN_TM
    const3 = lambda i: (0, 0, 0)
    return pl.pallas_call(
        functools.partial(_ffn_kernel, sub=sub, final_norm=final_norm),
        out_shape=jax.ShapeDtypeStruct((N, D), F32),
        grid=(N // FFN_TM,),
        in_specs=[
            pl.BlockSpec((FFN_TM, D), lambda i: (i, 0)),
            pl.BlockSpec((1,) + mod.shape[1:], lambda i: (i // bpb, 0, 0)),
            pl.BlockSpec((1, D), lambda i: (0, 0)),
            pl.BlockSpec(wa.shape, const3, pipeline_mode=pl.Buffered(1)),
            pl.BlockSpec(wb.shape, const3, pipeline_mode=pl.Buffered(1)),
            pl.BlockSpec(wd.shape, const3, pipeline_mode=pl.Buffered(1)),
            pl.BlockSpec((1, D), lambda i: (0, 0)),
        ],
        out_specs=pl.BlockSpec((FFN_TM, D), lambda i: (i, 0)),
        scratch_shapes=[pltpu.VMEM((FFN_TM, D), F32)],
        compiler_params=_cparams(("arbitrary",)),
        name="ffn_half",
    )(x, mod, nw.reshape(1, D), wa, wb, wd, fw.reshape(1, D))


def _inproj_kernel(x_ref, mod_ref, nw_ref, w_ref, o_ref, h_ref, *, sub):
    @pl.when(pl.program_id(1) == 0)
    def _():
        mod = mod_ref[0]
        shift, scale = mod[3 * sub:3 * sub + 1], mod[3 * sub + 1:3 * sub + 2]
        h_ref[...] = _modulated_norm(x_ref[...], nw_ref[...], shift, scale).astype(BF16)

    o_ref[...] = _dot(h_ref[...], w_ref[...])


def _inproj(x, mod, nw, w, *, sub, tokens_per_batch):
    N, D = x.shape
    M = w.shape[1]
    tm = min(INP_TM, tokens_per_batch)
    bpb = tokens_per_batch // tm
    return pl.pallas_call(
        functools.partial(_inproj_kernel, sub=sub),
        out_shape=jax.ShapeDtypeStruct((N, M), F32),
        grid=(N // tm, M // INP_TN),
        in_specs=[
            pl.BlockSpec((tm, D), lambda i, j: (i, 0)),
            pl.BlockSpec((1,) + mod.shape[1:], lambda i, j: (i // bpb, 0, 0)),
            pl.BlockSpec((1, D), lambda i, j: (0, 0)),
            pl.BlockSpec((D, INP_TN), lambda i, j: (0, j)),
        ],
        out_specs=pl.BlockSpec((tm, INP_TN), lambda i, j: (i, j)),
        scratch_shapes=[pltpu.VMEM((tm, D), BF16)],
        compiler_params=_cparams(("arbitrary", "arbitrary")),
        name="mixer_inproj",
    )(x, mod, nw.reshape(1, D), w)


def _mix_kernel(x_ref, mod_ref, yml_ref, yrw_ref, yhg_ref, g0_ref, g1_ref, g2_ref,
                wbr_ref, wout_ref, o_ref, *, sub):
    gate = mod_ref[0][3 * sub + 2:3 * sub + 3]
    mixed = None
    for n, (y_ref, g_ref) in enumerate(((yml_ref, g0_ref), (yrw_ref, g1_ref), (yhg_ref, g2_ref))):
        ybr = _dot(y_ref[...].astype(BF16), wbr_ref[n])
        term = _sigmoid(g_ref[...]) * ybr
        mixed = term if mixed is None else mixed + term
    o_ref[...] = x_ref[...] + (1.0 + gate) * _dot(mixed.astype(BF16), wout_ref[...])


def _branch_mix(x, mod, y_ml, y_rw, y_hg, z, gate_col0, w_branch, w_out, *, sub, tokens_per_batch):
    N, D = x.shape
    bpb = tokens_per_batch // MIX_TM
    g0 = gate_col0 // D
    tok = lambda i: (i, 0)
    return pl.pallas_call(
        functools.partial(_mix_kernel, sub=sub),
        out_shape=jax.ShapeDtypeStruct((N, D), F32),
        grid=(N // MIX_TM,),
        in_specs=[
            pl.BlockSpec((MIX_TM, D), tok),
            pl.BlockSpec((1,) + mod.shape[1:], lambda i: (i // bpb, 0, 0)),
            pl.BlockSpec((MIX_TM, D), tok),
            pl.BlockSpec((MIX_TM, D), tok),
            pl.BlockSpec((MIX_TM, D), tok),
            pl.BlockSpec((MIX_TM, D), lambda i: (i, g0)),
            pl.BlockSpec((MIX_TM, D), lambda i: (i, g0 + 1)),
            pl.BlockSpec((MIX_TM, D), lambda i: (i, g0 + 2)),
            pl.BlockSpec(w_branch.shape, lambda i: (0, 0, 0), pipeline_mode=pl.Buffered(1)),
            pl.BlockSpec(w_out.shape, lambda i: (0, 0), pipeline_mode=pl.Buffered(1)),
        ],
        out_specs=pl.BlockSpec((MIX_TM, D), tok),
        compiler_params=_cparams(("arbitrary",)),
        name="branch_mix",
    )(x, mod, y_ml, y_rw, y_hg, z, z, z, w_branch.astype(BF16), w_out.astype(BF16))


def _log_sigmoid(x):
    return jnp.minimum(x, 0.0) - jnp.log1p(jnp.exp(-jnp.abs(x)))


def _mlstm_kernel(bias_ref, x_ref, op_ref, g_ref, cw_ref, cb_ref, wq_ref, wk_ref,
                  nw_ref, sk_ref, y_ref, c_ref, n_ref, m_ref, tail_ref):
    L, W = x_ref.shape[1], x_ref.shape[2]
    H = ML_HEADS
    dh = W // H
    heads = range(H)
    sls = [slice(h * dh, (h + 1) * dh) for h in heads]

    @pl.when(pl.program_id(1) == 0)
    def _():
        c_ref[...] = jnp.zeros_like(c_ref)
        n_ref[...] = jnp.zeros_like(n_ref)
        m_ref[...] = jnp.zeros_like(m_ref)
        tail_ref[...] = jnp.zeros_like(tail_ref)

    x = x_ref[0]
    cw = cw_ref[...]
    tail = tail_ref[...]
    row8 = lax.broadcasted_iota(jnp.int32, (8, W), 0)
    conv = x * cw[ML_CONV - 1:ML_CONV]
    for s in range(1, ML_CONV):
        xr = pltpu.roll(x, s, 0)
        top = jnp.where(row8 < s, pltpu.roll(tail, s, 0), xr[0:8])
        xs = jnp.concatenate([top, xr[8:]], axis=0)
        conv = conv + xs * cw[ML_CONV - 1 - s:ML_CONV - s]
    tail_ref[...] = x[L - 8:L]
    xc = _silu(conv + cb_ref[...])
    xcb = xc.astype(BF16)
    vb = x.astype(BF16)

    g = g_ref[0]
    li = jnp.concatenate([g[h:h + 1] + bias_ref[h] for h in heads], axis=0)
    lf = _log_sigmoid(jnp.concatenate([g[H + h:H + h + 1] + bias_ref[H + h] for h in heads], axis=0))
    r = lax.broadcasted_iota(jnp.int32, (L, L), 0)
    c = lax.broadcasted_iota(jnp.int32, (L, L), 1)
    causal = r >= c
    b_rows = _dot_hi(lf, (r <= c).astype(F32))
    cols = jnp.concatenate([b_rows, li, jnp.zeros((LANES - 2 * H, L), F32)], axis=0).T

    q = [_dot(xcb[:, sl], wq_ref[h]) for h, sl in zip(heads, sls)]
    k = [_dot(xcb[:, sl], wk_ref[h]) * (dh ** -0.5) for h, sl in zip(heads, sls)]
    qb = [t.astype(BF16) for t in q]
    qk = [_dot_t(qb[h], k[h].astype(BF16)) for h in heads]
    c_prev = [c_ref[h] for h in heads]
    qc = [_dot(qb[h], c_prev[h].astype(BF16)) for h in heads]

    hs = []
    for h, sl in zip(heads, sls):
        b_row, li_row = b_rows[h:h + 1], li[h:h + 1]
        b_col, li_col = cols[:, h:h + 1], cols[:, H + h:H + h + 1]
        m_prev = m_ref[h:h + 1, 0:1]
        log_d = jnp.where(causal, b_col - b_row + li_row, NEG_BIG)
        log_inter = b_col + m_prev
        m_t = jnp.maximum(log_inter, jnp.max(log_d, axis=-1, keepdims=True))
        w_inter = jnp.exp(log_inter - m_t)
        s_mat = qk[h] * jnp.exp(log_d - m_t)
        n_row = n_ref[h:h + 1]
        num = w_inter * qc[h] + _dot(s_mat.astype(BF16), vb[:, sl])
        den = (w_inter * jnp.sum(q[h] * n_row, axis=-1, keepdims=True)
               + jnp.sum(s_mat, axis=-1, keepdims=True))
        hs.append(num / jnp.maximum(jnp.abs(den), jnp.exp(-m_t)))

        b_end = b_row[:, L - 1:L]
        log_ws = b_end - b_col + li_col
        m_new = jnp.maximum(b_end + m_prev, jnp.max(log_ws, axis=0, keepdims=True))
        decay = jnp.exp(b_end + m_prev - m_new)
        kw = k[h] * jnp.exp(log_ws - m_new)
        c_ref[h] = decay * c_prev[h] + _tdot(kw.astype(BF16), vb[:, sl])
        n_ref[h:h + 1] = decay * n_row + jnp.sum(kw, axis=0, keepdims=True)
        m_ref[h:h + 1] = jnp.broadcast_to(m_new, (1, LANES))

    gate = _sigmoid(op_ref[0])
    nw, sk = nw_ref[...], sk_ref[...]
    for h, sl in zip(heads, sls):
        hg = hs[h] * gate[:, sl]
        hg = hg - jnp.mean(hg, axis=-1, keepdims=True)
        hn = hg * lax.rsqrt(jnp.mean(hg * hg, axis=-1, keepdims=True) + EPS)
        y_ref[0, :, sl] = hn * nw[:, sl] + sk[:, sl] * xc[:, sl]


def _mlstm(z3, gates, bias, conv_w, conv_b, wq, wk, norm_w, skip, *, width):
    B, T, _ = z3.shape
    dh = width // ML_HEADS
    L = min(ML_CHUNK, T)
    vec = lambda b, t: (0, 0)
    return pl.pallas_call(
        _mlstm_kernel,
        out_shape=jax.ShapeDtypeStruct((B, T, width), F32),
        grid=(B, T // L),
        in_specs=[
            pl.BlockSpec(memory_space=pltpu.SMEM),
            pl.BlockSpec((1, L, width), lambda b, t: (b, t, 0)),
            pl.BlockSpec((1, L, width), lambda b, t: (b, t, 1)),
            pl.BlockSpec((1, 2 * ML_HEADS, L), lambda b, t: (b, 0, t)),
            pl.BlockSpec((ML_CONV, width), vec),
            pl.BlockSpec((1, width), vec),
            pl.BlockSpec((ML_HEADS, dh, dh), lambda b, t: (0, 0, 0)),
            pl.BlockSpec((ML_HEADS, dh, dh), lambda b, t: (0, 0, 0)),
            pl.BlockSpec((1, width), vec),
            pl.BlockSpec((1, width), vec),
        ],
        out_specs=pl.BlockSpec((1, L, width), lambda b, t: (b, t, 0)),
        scratch_shapes=[
            pltpu.VMEM((ML_HEADS, dh, dh), F32),
            pltpu.VMEM((ML_HEADS, dh), F32),
            pltpu.VMEM((ML_HEADS, LANES), F32),
            pltpu.VMEM((8, width), F32),
        ],
        compiler_params=_cparams(("arbitrary", "arbitrary")),
        name="mlstm",
    )(bias, z3, z3, gates, conv_w, conv_b.reshape(1, width), wq.astype(BF16),
      wk.astype(BF16), norm_w.reshape(1, width), skip.reshape(1, width))


def _midrow_broadcast(b, m):
    n, w = b.shape
    row = lambda i: b[i:i + 1]
    if m >= 8:
        return jnp.concatenate(
            [jnp.broadcast_to(row(p * 2 * m + m), (2 * m, w)) for p in range(n // (2 * m))], axis=0)
    pick = lambda off: jnp.concatenate(
        [jnp.broadcast_to(row(8 * g + off), (8, w)) for g in range(n // 8)], axis=0)
    sub = lax.broadcasted_iota(jnp.int32, (n, w), 0) % 8
    if m == 4:
        return pick(4)
    if m == 2:
        return jnp.where(sub < 4, pick(2), pick(6))
    return jnp.where(sub < 2, pick(1), jnp.where(sub < 4, pick(3), jnp.where(sub < 6, pick(5), pick(7))))


def _hgrn_kernel(q_ref, f_ref, i_ref, g_ref, lb_ref, nw_ref, y_ref, s_ref):
    C, W = q_ref.shape[1], q_ref.shape[2]
    H = W // HG_EXPAND

    @pl.when(pl.program_id(1) == 0)
    def _():
        s_ref[...] = jnp.zeros_like(s_ref)

    lb = lb_ref[...]
    sig = _sigmoid(f_ref[0])
    q = _silu(q_ref[0])
    k = (1.0 - lb) * (1.0 - sig)
    log_f = jnp.log(jnp.maximum(lb + (1.0 - lb) * sig, TINY))
    r = lax.broadcasted_iota(jnp.int32, (C, C), 0)
    c = lax.broadcasted_iota(jnp.int32, (C, C), 1)
    b = _dot_hi((r >= c).astype(F32), log_f)
    v = i_ref[0]
    vb = v.astype(BF16)

    rowi = lax.broadcasted_iota(jnp.int32, (C, W), 0)
    xs, ys, masks = [], [], []
    m = 1
    while m < C:
        e = jnp.exp(-jnp.abs(b - _midrow_broadcast(b, m)))
        second = (rowi % (2 * m)) >= m
        xs.append(jnp.where(second, q * e, 0.0).astype(BF16))
        ys.append(jnp.where(second, 0.0, k * e).astype(BF16))
        masks.append((r // (2 * m)) == (c // (2 * m)))
        m *= 2

    qe = (q * jnp.exp(b)).astype(BF16)
    b_end = b[C - 1:C]
    kd = (k * jnp.exp(b_end - b)).astype(BF16)
    g_end = jnp.exp(b_end)
    diag = q * k
    gate = _sigmoid(g_ref[0])
    nw = nw_ref[...]
    heads = range(H)
    sls = [slice(h * HG_EXPAND, (h + 1) * HG_EXPAND) for h in heads]
    st = [s_ref[h] for h in heads]
    inter = [_dot_t(qe[:, sl], st[h].astype(BF16)) for h, sl in zip(heads, sls)]
    for h, sl in zip(heads, sls):
        s_ref[h] = st[h] * g_end[:, sl] + _tdot(vb[:, sl], kd[:, sl])
    a = [None] * H
    for x_l, y_l, mask in zip(xs, ys, masks):
        for h, sl in zip(heads, sls):
            a_l = jnp.where(mask, _dot_t(x_l[:, sl], y_l[:, sl]), 0.0)
            a[h] = a_l if a[h] is None else a[h] + a_l
    for h, sl in zip(heads, sls):
        o = (inter[h] + _dot(a[h].astype(BF16), vb[:, sl])
             + jnp.sum(diag[:, sl], axis=-1, keepdims=True) * v[:, sl])
        on = o * lax.rsqrt(jnp.mean(o * o, axis=-1, keepdims=True) + EPS)
        y_ref[0, :, sl] = on * nw[:, sl] * gate[:, sl]


def _hgrn(z3, lower_bound, norm_w, *, col0, width):
    B, T, _ = z3.shape
    C = min(HG_CHUNK, T)
    H = width // HG_EXPAND
    blk0 = col0 // width
    spec = lambda j: pl.BlockSpec((1, C, width), lambda b, t: (b, t, blk0 + j))
    vec = pl.BlockSpec((1, width), lambda b, t: (0, 0))
    return pl.pallas_call(
        _hgrn_kernel,
        out_shape=jax.ShapeDtypeStruct((B, T, width), F32),
        grid=(B, T // C),
        in_specs=[spec(0), spec(1), spec(2), spec(3), vec, vec],
        out_specs=pl.BlockSpec((1, C, width), lambda b, t: (b, t, 0)),
        scratch_shapes=[pltpu.VMEM((H, HG_EXPAND, HG_EXPAND), F32)],
        compiler_params=_cparams(("arbitrary", "arbitrary")),
        name="hgrn2",
    )(z3, z3, z3, z3, lower_bound.reshape(1, width), norm_w.reshape(1, width))


def _token_shift(z, mu, last_ref):
    n = z.shape[0]
    row = lax.broadcasted_iota(jnp.int32, z.shape, 0)
    prev = jnp.where(row == 0, last_ref[7:8], pltpu.roll(z, 1, 0))
    last_ref[...] = z[n - 8:n]
    return z + mu * (prev - z)


def _lane_groups_to_rows(x):
    return jnp.concatenate([x[:, g * LANES:(g + 1) * LANES] for g in range(x.shape[1] // LANES)], axis=0)


def _head_sum(x, ones_bd):
    n, w = x.shape
    xs = _lane_groups_to_rows(x)
    hi = xs.astype(BF16)
    lo = (xs - hi.astype(F32)).astype(BF16)
    s = _dot(hi, ones_bd) + _dot(lo, ones_bd)
    return jnp.concatenate([s[g * n:(g + 1) * n] for g in range(w // LANES)], axis=1)


def _softplus(x):
    return jnp.maximum(x, 0.0) + jnp.log1p(jnp.exp(-jnp.abs(x)))


def _rwkv_kernel(*refs, has_vres):
    it = iter(refs)
    r_ref, k_ref, v_ref, lo_ref = next(it), next(it), next(it), next(it)
    sm_ref, vf_ref = (next(it), next(it)) if has_vres else (None, None)
    mu_ref, mulo_ref, vec_ref, w2_ref, a2_ref, g2_ref = (next(it) for _ in range(6))
    musm_ref, v2_ref = (next(it), next(it)) if has_vres else (None, None)
    y_ref = next(it)
    vfo_ref = None if has_vres else next(it)
    s_ref, pr_ref, pk_ref, pv_ref, plo_ref = (next(it) for _ in range(5))
    psm_ref = next(it) if has_vres else None

    C, W = r_ref.shape[1], r_ref.shape[2]
    P = W // LANES

    @pl.when(pl.program_id(1) == 0)
    def _():
        s_ref[...] = jnp.zeros_like(s_ref)
        for ref in (pr_ref, pk_ref, pv_ref, plo_ref, psm_ref):
            if ref is not None:
                ref[...] = jnp.zeros_like(ref)

    vec = vec_ref[...]
    w0, a0, k_k, k_a, ln_w, ln_b, r_k, v0 = (vec[i:i + 1] for i in range(8))
    mu = mu_ref[...]
    r = _token_shift(r_ref[0], mu[0:1], pr_ref)
    k = _token_shift(k_ref[0], mu[1:2], pk_ref)
    v = _token_shift(v_ref[0], mu[2:3], pv_ref)
    lo = _token_shift(lo_ref[0], mulo_ref[...], plo_ref)
    lo_wa, lo_g = lo[:, :LANES], lo[:, LANES:]

    w_raw = -_softplus(-(w0 + _dot(jnp.tanh(lo_wa).astype(BF16), w2_ref[...]))) - 0.5
    lw = -jnp.exp(w_raw)
    a = _sigmoid(a0 + _dot(lo_wa.astype(BF16), a2_ref[...]))
    g = _dot(_sigmoid(lo_g).astype(BF16), g2_ref[...])
    if has_vres:
        sm = _token_shift(sm_ref[0], musm_ref[...], psm_ref)
        v = v + (vf_ref[0] - v) * _sigmoid(v0 + _dot(sm.astype(BF16), v2_ref[...]))
    else:
        vfo_ref[0] = v

    rr = lax.broadcasted_iota(jnp.int32, (LANES, LANES), 0)
    cc = lax.broadcasted_iota(jnp.int32, (LANES, LANES), 1)
    ones_bd = ((rr // RW_HEAD) == (cc // RW_HEAD)).astype(BF16)

    kk = k * k_k
    kk = kk / jnp.maximum(jnp.sqrt(_head_sum(kk * kk, ones_bd)), 1e-12)
    kmod = k * (1.0 + (a - 1.0) * k_a)
    ahat = kk * a

    tr = lax.broadcasted_iota(jnp.int32, (C, C), 0)
    tc = lax.broadcasted_iota(jnp.int32, (C, C), 1)
    cum = _dot_hi((tr >= tc).astype(F32), lw)
    e_inc = jnp.exp(cum)
    e_inv = jnp.exp(-cum)
    r_dec = r * e_inc
    kk_dec = kk * jnp.exp(cum - lw)
    a_inv = ahat * e_inv
    k_inv = kmod * e_inv
    gamma_end = e_inc[C - 1:C]

    lane = lax.broadcasted_iota(jnp.int32, (C, LANES), 1)
    first = lane < RW_HEAD

    def stack(zp):
        return jnp.concatenate([jnp.where(first, zp, 0.0), jnp.where(first, 0.0, zp)], axis=0).astype(BF16)

    strict = rr > cc
    incl = rr >= cc
    eye = (rr == cc).astype(F32)
    n_sq = int(np.log2(C)) - 1
    pairs = range(P)
    n2 = 2 * C
    sls = [slice(p * LANES, (p + 1) * LANES) for p in pairs]
    kks = [stack(kk_dec[:, sl]) for sl in sls]
    rs = [stack(r_dec[:, sl]) for sl in sls]
    vs = [stack(v[:, sl]) for sl in sls]
    ak = [jnp.concatenate([stack(a_inv[:, sl]), stack(k_inv[:, sl])], axis=0) for sl in sls]
    sc = [_dot_t(jnp.concatenate([kks[p], rs[p]], axis=0), ak[p]) for p in pairs]
    l_a = [jnp.where(strict, sc[p][:n2, :n2], 0.0) for p in pairs]
    lkv = [_dot(jnp.where(strict, sc[p][:n2, n2:], 0.0).astype(BF16), vs[p]) for p in pairs]
    a_cat = [jnp.where(jnp.concatenate([incl, incl], axis=1), sc[p][n2:], 0.0).astype(BF16) for p in pairs]
    t_inv = [eye - l_a[p] for p in pairs]
    pw = [l_a[p].astype(BF16) for p in pairs]
    for _ in range(n_sq):
        pw = [_dot(pw[p], pw[p]).astype(BF16) for p in pairs]
        t_inv = [t_inv[p] + _dot(t_inv[p].astype(BF16), pw[p]) for p in pairs]
    rhs = [jnp.concatenate([-kks[p].astype(F32), -lkv[p]], axis=1).astype(BF16) for p in pairs]
    wv = [_dot(t_inv[p].astype(BF16), rhs[p]) for p in pairs]
    st = [s_ref[p] for p in pairs]
    wr = [_dot_t(jnp.concatenate([wv[p][:, :LANES].astype(BF16), rs[p]], axis=0), st[p].astype(BF16))
          for p in pairs]
    uv = [jnp.concatenate([(wr[p][:n2] + wv[p][:, LANES:]).astype(BF16), vs[p]], axis=0) for p in pairs]
    yst = [wr[p][n2:] + _dot(a_cat[p], uv[p]) for p in pairs]
    for p in pairs:
        s_ref[p] = (st[p] + _tdot(uv[p], ak[p])) * gamma_end[:, sls[p]]
    y = jnp.concatenate([yst[p][:C] + yst[p][C:] for p in pairs], axis=1)

    inv_n = 1.0 / RW_HEAD
    yc = y - _head_sum(y, ones_bd) * inv_n
    yn = yc * lax.rsqrt(_head_sum(yc * yc, ones_bd) * inv_n + RW_LN_EPS) * ln_w + ln_b
    bonus = _head_sum(r * kmod * r_k, ones_bd) * v
    y_ref[0] = (yn + bonus) * g


def _rwkv(z3, v_first, prm, *, rkv_col0, lora_col0, small_col0, width):
    B, T, _ = z3.shape
    C = min(RW_CHUNK, T)
    has_vres = v_first is not None
    lora_w = RW_DECAY_LORA + RW_AAA_LORA + RW_GATE_LORA
    tok = lambda wdt, col0, j=0: pl.BlockSpec((1, C, wdt), lambda b, t: (b, t, col0 // wdt + j))
    full = lambda arr: pl.BlockSpec(arr.shape, lambda b, t: (0,) * arr.ndim)
    tok_out = pl.BlockSpec((1, C, width), lambda b, t: (b, t, 0))

    args = [z3, z3, z3, z3]
    specs = [tok(width, rkv_col0, 0), tok(width, rkv_col0, 1), tok(width, rkv_col0, 2), tok(lora_w, lora_col0)]
    if has_vres:
        args += [z3, v_first]
        specs += [tok(LANES, small_col0), tok_out]
    weights = [prm["mu_rkv"], prm["mu_lora"], prm["vecs"], prm["w2"], prm["a2"], prm["g2"]]
    if has_vres:
        weights += [prm["mu_small"], prm["v2"]]
    args += weights
    specs += [full(wt) for wt in weights]

    y_shape = jax.ShapeDtypeStruct((B, T, width), F32)
    scratch = [pltpu.VMEM((width // LANES, LANES, LANES), F32)]
    scratch += [pltpu.VMEM((8, width), F32)] * 3 + [pltpu.VMEM((8, lora_w), F32)]
    if has_vres:
        scratch += [pltpu.VMEM((8, LANES), F32)]
    out = pl.pallas_call(
        functools.partial(_rwkv_kernel, has_vres=has_vres),
        out_shape=y_shape if has_vres else (y_shape, y_shape),
        grid=(B, T // C),
        in_specs=specs,
        out_specs=tok_out if has_vres else (tok_out, tok_out),
        scratch_shapes=scratch,
        compiler_params=_cparams(("arbitrary", "arbitrary")),
        name="rwkv7",
    )(*args)
    return (out, v_first) if has_vres else out


def _pad_rows(w, row0, rows):
    return jnp.zeros((rows, w.shape[1]), w.dtype).at[row0:row0 + w.shape[0]].set(w)


def kernel(x, c, norm_w, final_norm_w, ada_w, ada_b, ffn_up, ffn_down, w_in, w_in_vres,
           ml_conv_w, ml_conv_b, ml_wq, ml_wk, ml_i_b, ml_f_b, ml_norm_w, ml_skip,
           rw_mu, rw_mu_vres, rw_w0, rw_w2, rw_a0, rw_a2, rw_v0, rw_v2, rw_g2,
           rw_k_k, rw_k_a, rw_r_k, rw_ln_w, rw_ln_b, hg_lb_logits, hg_norm_w,
           w_branch, w_out):
    B, T, D = x.shape
    depth = norm_w.shape[0]
    N = B * T
    W = D
    lora_w = RW_DECAY_LORA + RW_AAA_LORA + RW_GATE_LORA

    o_mlx, o_mlo, o_mli, o_mlf = 0, W, 2 * W, 2 * W + ML_HEADS
    o_rw = 2 * W + 2 * ML_HEADS
    o_hg = o_rw + 3 * W + lora_w
    o_gate = o_hg + 4 * W
    n_in = o_gate + N_BRANCH * D
    c_rkv, c_hg, c_gate = 2 * W, 5 * W, 9 * W
    c_lora = c_gate + N_BRANCH * D
    c_small = c_lora + lora_w
    n_cols = c_small + LANES
    s_vlo = 2 * ML_HEADS

    lb_soft = jax.nn.softmax(hg_lb_logits.astype(F32), axis=0)
    lower_bounds = jnp.cumsum(lb_soft, axis=0) - lb_soft[0]

    mod_all = _ada_mod(c, ada_w, ada_b).reshape(depth, B, N_SUB * 3, D)
    x2 = x.reshape(N, D)
    v_first = None
    for l in range(depth):
        mod = mod_all[l]
        wl = w_in[l]
        small = jnp.concatenate([wl[:, o_mli:o_mli + 2 * ML_HEADS]]
                                + ([w_in_vres[l - 1]] if l > 0 else []), axis=1)
        small = jnp.pad(small, ((0, 0), (0, LANES - small.shape[1])))
        w_cat = jnp.concatenate([
            wl[:, o_mlx:o_mlx + 2 * W], wl[:, o_rw:o_rw + 3 * W], wl[:, o_hg:o_hg + 4 * W],
            wl[:, o_gate:n_in], wl[:, o_rw + 3 * W:o_rw + 3 * W + lora_w], small], axis=1).astype(BF16)

        x2 = _ffn(x2, mod, norm_w[l, 0], ffn_up[l, 0], ffn_down[l, 0], final_norm_w,
                  sub=0, tokens_per_batch=T, final_norm=False)

        z = _inproj(x2, mod, norm_w[l, 1], w_cat, sub=1, tokens_per_batch=T)
        z3 = z.reshape(B, T, n_cols)

        gates = z3[:, :, c_small:c_small + 2 * ML_HEADS].transpose(0, 2, 1)
        y_ml = _mlstm(z3, gates, jnp.concatenate([ml_i_b[l], ml_f_b[l]]), ml_conv_w[l], ml_conv_b[l],
                      ml_wq[l], ml_wk[l], ml_norm_w[l], ml_skip[l], width=W)

        mu = rw_mu[l]
        prm = {
            "mu_rkv": mu[:3 * W].reshape(3, W),
            "mu_lora": mu[3 * W:].reshape(1, lora_w),
            "vecs": jnp.stack([rw_w0[l], rw_a0[l], rw_k_k[l], rw_k_a[l], rw_ln_w[l], rw_ln_b[l],
                               rw_r_k[l].reshape(W), rw_v0[l - 1] if l > 0 else jnp.zeros((W,), F32)]),
            "w2": _pad_rows(rw_w2[l], 0, LANES).astype(BF16),
            "a2": _pad_rows(rw_a2[l], RW_DECAY_LORA, LANES).astype(BF16),
            "g2": rw_g2[l].astype(BF16),
        }
        if l > 0:
            prm["mu_small"] = jnp.zeros((1, LANES), F32).at[0, s_vlo:s_vlo + RW_MV_LORA].set(rw_mu_vres[l - 1])
            prm["v2"] = _pad_rows(rw_v2[l - 1], s_vlo, LANES).astype(BF16)
        y_rw, v_first = _rwkv(z3, v_first, prm, rkv_col0=c_rkv, lora_col0=c_lora,
                              small_col0=c_small, width=W)

        y_hg = _hgrn(z3, lower_bounds[l], hg_norm_w[l], col0=c_hg, width=W)

        x2 = _branch_mix(x2, mod, y_ml.reshape(N, W), y_rw.reshape(N, W), y_hg.reshape(N, W), z,
                         c_gate, w_branch[l], w_out[l], sub=1, tokens_per_batch=T)

        x2 = _ffn(x2, mod, norm_w[l, 2], ffn_up[l, 1], ffn_down[l, 1], final_norm_w,
                  sub=2, tokens_per_batch=T, final_norm=(l == depth - 1))
    return x2.reshape(B, T, D)
```

```python
import functools

import numpy as np
import jax
import jax.numpy as jnp
from jax import lax
from jax.experimental import pallas as pl
from jax.experimental.pallas import tpu as pltpu

F32 = jnp.float32
BF16 = jnp.bfloat16

EPS = 1e-6
NEG_BIG = -1e30
TINY = 1e-30

N_SUB = 3
N_BRANCH = 3
ML_HEADS = 4
ML_CONV = 4
RW_HEAD = 64
RW_LN_EPS = 64e-5
RW_DECAY_LORA = 64
RW_AAA_LORA = 64
RW_MV_LORA = 32
RW_GATE_LORA = 128
HG_EXPAND = 128

LANES = 128
VMEM_LIMIT = 56 * 1024 * 1024

FFN_TM = 512
FFN_FC = 256
INP_TM = 1024
INP_TN = 1920
MIX_TM = 512
ML_CHUNK = 256
RW_CHUNK = 64
HG_CHUNK = 128


def _cparams(sem):
    return pltpu.CompilerParams(dimension_semantics=sem, vmem_limit_bytes=VMEM_LIMIT)


def _dot(a, b):
    return jnp.dot(a, b, preferred_element_type=F32)


def _dot_t(a, b):
    return lax.dot_general(a, b, (((1,), (1,)), ((), ())), preferred_element_type=F32)


def _tdot(a, b):
    return lax.dot_general(a, b, (((0,), (0,)), ((), ())), preferred_element_type=F32)


def _split3(x):
    hi = x.astype(BF16)
    r1 = x - hi.astype(F32)
    mid = r1.astype(BF16)
    lo = (r1 - mid.astype(F32)).astype(BF16)
    return hi, mid, lo


def _cumsum_rows(tri, x):
    hi, mid, lo = _split3(x)
    return _dot(tri, hi) + (_dot(tri, mid) + _dot(tri, lo))


def _cumsum_lanes(x, tri):
    hi, mid, lo = _split3(x)
    return _dot(hi, tri) + (_dot(mid, tri) + _dot(lo, tri))


def _sigmoid(x):
    return 1.0 / (1.0 + jnp.exp(-x))


def _gate_sigmoid(x):
    return 0.5 * jnp.tanh(0.5 * x) + 0.5


def _silu(x):
    return x * _gate_sigmoid(x)


def _modulated_norm(x, nw, shift, scale):
    ms = jnp.mean(x * x, axis=-1, keepdims=True)
    return (x * lax.rsqrt(ms + EPS) * nw) * (1.0 + scale) + shift


def _ada_kernel(c_ref, w_ref, b_ref, o_ref):
    cond = _silu(c_ref[...])
    o_ref[0] = _dot(cond, w_ref[0]) + b_ref[0]


def _ada_mod(c, ada_w, ada_b):
    L, D, M = ada_w.shape
    B = c.shape[0]
    tn = M // 8
    return pl.pallas_call(
        _ada_kernel,
        out_shape=jax.ShapeDtypeStruct((L, B, M), F32),
        grid=(L, M // tn),
        in_specs=[
            pl.BlockSpec((B, D), lambda l, j: (0, 0)),
            pl.BlockSpec((1, D, tn), lambda l, j: (l, 0, j)),
            pl.BlockSpec((1, 1, tn), lambda l, j: (l, 0, j)),
        ],
        out_specs=pl.BlockSpec((1, B, tn), lambda l, j: (l, 0, j)),
        compiler_params=_cparams(("arbitrary", "arbitrary")),
        name="ada_mod",
    )(c, ada_w, ada_b.reshape(L, 1, M))


def _ffn_kernel(x_ref, mod_ref, nw_ref, wa_ref, wb_ref, wd_ref, fw_ref, o_ref, acc_ref,
                *, sub, final_norm):
    x = x_ref[...]
    mod = mod_ref[0]
    shift, scale, gate = (mod[3 * sub + i:3 * sub + i + 1] for i in range(3))
    hb = _modulated_norm(x, nw_ref[...], shift, scale).astype(BF16)
    for c in range(wa_ref.shape[0]):
        a = _dot(hb, wa_ref[c])
        b = _dot(hb, wb_ref[c])
        act = (_silu(a) * b).astype(BF16)
        d = _dot(act, wd_ref[c])
        if c == 0:
            acc_ref[...] = d
        else:
            acc_ref[...] += d
    y = x + (0.5 * (1.0 + gate)) * acc_ref[...]
    if final_norm:
        ms = jnp.mean(y * y, axis=-1, keepdims=True)
        y = y * lax.rsqrt(ms + EPS) * fw_ref[...]
    o_ref[...] = y


def _ffn(x, mod, nw, w_up, w_down, fw, *, sub, tokens_per_batch, final_norm):
    N, D = x.shape
    F = w_down.shape[0]
    nc = F // FFN_FC
    wa = w_up[:, :F].astype(BF16).reshape(D, nc, FFN_FC).transpose(1, 0, 2)
    wb = w_up[:, F:].astype(BF16).reshape(D, nc, FFN_FC).transpose(1, 0, 2)
    wd = w_down.astype(BF16).reshape(nc, FFN_FC, D)
    bpb = tokens_per_batch // FFN_TM
    const3 = lambda i: (0, 0, 0)
    return pl.pallas_call(
        functools.partial(_ffn_kernel, sub=sub, final_norm=final_norm),
        out_shape=jax.ShapeDtypeStruct((N, D), F32),
        grid=(N // FFN_TM,),
        in_specs=[
            pl.BlockSpec((FFN_TM, D), lambda i: (i, 0)),
            pl.BlockSpec((1,) + mod.shape[1:], lambda i: (i // bpb, 0, 0)),
            pl.BlockSpec((1, D), lambda i: (0, 0)),
            pl.BlockSpec(wa.shape, const3, pipeline_mode=pl.Buffered(1)),
            pl.BlockSpec(wb.shape, const3, pipeline_mode=pl.Buffered(1)),
            pl.BlockSpec(wd.shape, const3, pipeline_mode=pl.Buffered(1)),
            pl.BlockSpec((1, D), lambda i: (0, 0)),
        ],
        out_specs=pl.BlockSpec((FFN_TM, D), lambda i: (i, 0)),
        scratch_shapes=[pltpu.VMEM((FFN_TM, D), F32)],
        compiler_params=_cparams(("arbitrary",)),
        name="ffn_half",
    )(x, mod, nw.reshape(1, D), wa, wb, wd, fw.reshape(1, D))


def _inproj_kernel(x_ref, mod_ref, nw_ref, w_ref, o_ref, h_ref, *, sub):
    @pl.when(pl.program_id(1) == 0)
    def _():
        mod = mod_ref[0]
        shift, scale = mod[3 * sub:3 * sub + 1], mod[3 * sub + 1:3 * sub + 2]
        h_ref[...] = _modulated_norm(x_ref[...], nw_ref[...], shift, scale).astype(BF16)

    o_ref[...] = _dot(h_ref[...], w_ref[...])


def _inproj(x, mod, nw, w, *, sub, tokens_per_batch):
    N, D = x.shape
    M = w.shape[1]
    tm = min(INP_TM, tokens_per_batch)
    bpb = tokens_per_batch // tm
    return pl.pallas_call(
        functools.partial(_inproj_kernel, sub=sub),
        out_shape=jax.ShapeDtypeStruct((N, M), F32),
        grid=(N // tm, M // INP_TN),
        in_specs=[
            pl.BlockSpec((tm, D), lambda i, j: (i, 0)),
            pl.BlockSpec((1,) + mod.shape[1:], lambda i, j: (i // bpb, 0, 0)),
            pl.BlockSpec((1, D), lambda i, j: (0, 0)),
            pl.BlockSpec((D, INP_TN), lambda i, j: (0, j)),
        ],
        out_specs=pl.BlockSpec((tm, INP_TN), lambda i, j: (i, j)),
        scratch_shapes=[pltpu.VMEM((tm, D), BF16)],
        compiler_params=_cparams(("arbitrary", "arbitrary")),
        name="mixer_inproj",
    )(x, mod, nw.reshape(1, D), w)


def _mix_kernel(x_ref, mod_ref, nw_ref, yml_ref, yrw_ref, yhg_ref, wg_ref, wbr_ref, wout_ref, o_ref,
                *, sub):
    x = x_ref[...]
    mod = mod_ref[0]
    shift, scale, gate = (mod[3 * sub + i:3 * sub + i + 1] for i in range(3))
    hb = _modulated_norm(x, nw_ref[...], shift, scale).astype(BF16)
    mixed = None
    for n, y_ref in enumerate((yml_ref, yrw_ref, yhg_ref)):
        term = _gate_sigmoid(_dot(hb, wg_ref[n])) * _dot(y_ref[...].astype(BF16), wbr_ref[n])
        mixed = term if mixed is None else mixed + term
    o_ref[...] = x + (1.0 + gate) * _dot(mixed.astype(BF16), wout_ref[...])


def _branch_mix(x, mod, nw, y_ml, y_rw, y_hg, w_gate, w_branch, w_out, *, sub, tokens_per_batch):
    N, D = x.shape
    bpb = tokens_per_batch // MIX_TM
    tok = pl.BlockSpec((MIX_TM, D), lambda i: (i, 0))
    resident = lambda w: pl.BlockSpec(w.shape, lambda i: (0,) * w.ndim, pipeline_mode=pl.Buffered(1))
    wg = w_gate.astype(BF16).reshape(D, N_BRANCH, D).transpose(1, 0, 2)
    return pl.pallas_call(
        functools.partial(_mix_kernel, sub=sub),
        out_shape=jax.ShapeDtypeStruct((N, D), F32),
        grid=(N // MIX_TM,),
        in_specs=[
            tok,
            pl.BlockSpec((1,) + mod.shape[1:], lambda i: (i // bpb, 0, 0)),
            pl.BlockSpec((1, D), lambda i: (0, 0)),
            tok, tok, tok,
            resident(wg), resident(w_branch), resident(w_out),
        ],
        out_specs=tok,
        compiler_params=_cparams(("arbitrary",)),
        name="branch_mix",
    )(x, mod, nw.reshape(1, D), y_ml, y_rw, y_hg, wg, w_branch.astype(BF16), w_out.astype(BF16))


def _log_sigmoid(x):
    return jnp.minimum(x, 0.0) - jnp.log1p(jnp.exp(-jnp.abs(x)))


def _mlstm_kernel(bias_ref, x_ref, op_ref, g_ref, cw_ref, cb_ref, wq_ref, wk_ref,
                  nw_ref, sk_ref, y_ref, c_ref, n_ref, m_ref, tail_ref):
    L, W = x_ref.shape[1], x_ref.shape[2]
    H = ML_HEADS
    dh = W // H
    heads = range(H)
    sls = [slice(h * dh, (h + 1) * dh) for h in heads]

    @pl.when(pl.program_id(1) == 0)
    def _():
        c_ref[...] = jnp.zeros_like(c_ref)
        n_ref[...] = jnp.zeros_like(n_ref)
        m_ref[...] = jnp.zeros_like(m_ref)
        tail_ref[...] = jnp.zeros_like(tail_ref)

    x = x_ref[0]
    cw = cw_ref[...]
    tail = tail_ref[...]
    row8 = lax.broadcasted_iota(jnp.int32, (8, W), 0)
    conv = x * cw[ML_CONV - 1:ML_CONV]
    for s in range(1, ML_CONV):
        xr = pltpu.roll(x, s, 0)
        top = jnp.where(row8 < s, pltpu.roll(tail, s, 0), xr[0:8])
        xs = jnp.concatenate([top, xr[8:]], axis=0)
        conv = conv + xs * cw[ML_CONV - 1 - s:ML_CONV - s]
    tail_ref[...] = x[L - 8:L]
    xc = _silu(conv + cb_ref[...])
    xcb = xc.astype(BF16)
    vb = x.astype(BF16)

    g = g_ref[0]
    li = jnp.concatenate([g[h:h + 1] + bias_ref[h] for h in heads], axis=0)
    lf = _log_sigmoid(jnp.concatenate([g[H + h:H + h + 1] + bias_ref[H + h] for h in heads], axis=0))
    r = lax.broadcasted_iota(jnp.int32, (L, L), 0)
    c = lax.broadcasted_iota(jnp.int32, (L, L), 1)
    causal = r >= c
    b_rows = _cumsum_lanes(lf, (r <= c).astype(BF16))
    cols = jnp.concatenate([b_rows, li, jnp.zeros((LANES - 2 * H, L), F32)], axis=0).T

    q = [_dot(xcb[:, sl], wq_ref[h]) for h, sl in zip(heads, sls)]
    k = [_dot(xcb[:, sl], wk_ref[h]) * (dh ** -0.5) for h, sl in zip(heads, sls)]
    qb = [t.astype(BF16) for t in q]
    qk = [_dot_t(qb[h], k[h].astype(BF16)) for h in heads]
    c_prev = [c_ref[h] for h in heads]
    qc = [_dot(qb[h], c_prev[h].astype(BF16)) for h in heads]

    hs = []
    for h, sl in zip(heads, sls):
        b_row, li_row = b_rows[h:h + 1], li[h:h + 1]
        b_col, li_col = cols[:, h:h + 1], cols[:, H + h:H + h + 1]
        m_prev = m_ref[h:h + 1, 0:1]
        log_d = jnp.where(causal, b_col - b_row + li_row, NEG_BIG)
        log_inter = b_col + m_prev
        m_t = jnp.maximum(log_inter, jnp.max(log_d, axis=-1, keepdims=True))
        w_inter = jnp.exp(log_inter - m_t)
        s_mat = qk[h] * jnp.exp(log_d - m_t)
        n_row = n_ref[h:h + 1]
        num = w_inter * qc[h] + _dot(s_mat.astype(BF16), vb[:, sl])
        den = (w_inter * jnp.sum(q[h] * n_row, axis=-1, keepdims=True)
               + jnp.sum(s_mat, axis=-1, keepdims=True))
        hs.append(num / jnp.maximum(jnp.abs(den), jnp.exp(-m_t)))

        b_end = b_row[:, L - 1:L]
        log_ws = b_end - b_col + li_col
        m_new = jnp.maximum(b_end + m_prev, jnp.max(log_ws, axis=0, keepdims=True))
        decay = jnp.exp(b_end + m_prev - m_new)
        kw = k[h] * jnp.exp(log_ws - m_new)
        c_ref[h] = decay * c_prev[h] + _tdot(kw.astype(BF16), vb[:, sl])
        n_ref[h:h + 1] = decay * n_row + jnp.sum(kw, axis=0, keepdims=True)
        m_ref[h:h + 1] = jnp.broadcast_to(m_new, (1, LANES))

    gate = _gate_sigmoid(op_ref[0])
    nw, sk = nw_ref[...], sk_ref[...]
    for h, sl in zip(heads, sls):
        hg = hs[h] * gate[:, sl]
        hg = hg - jnp.mean(hg, axis=-1, keepdims=True)
        hn = hg * lax.rsqrt(jnp.mean(hg * hg, axis=-1, keepdims=True) + EPS)
        y_ref[0, :, sl] = hn * nw[:, sl] + sk[:, sl] * xc[:, sl]


def _mlstm(z3, gates, bias, conv_w, conv_b, wq, wk, norm_w, skip, *, width):
    B, T, _ = z3.shape
    dh = width // ML_HEADS
    L = min(ML_CHUNK, T)
    vec = lambda b, t: (0, 0)
    return pl.pallas_call(
        _mlstm_kernel,
        out_shape=jax.ShapeDtypeStruct((B, T, width), F32),
        grid=(B, T // L),
        in_specs=[
            pl.BlockSpec(memory_space=pltpu.SMEM),
            pl.BlockSpec((1, L, width), lambda b, t: (b, t, 0)),
            pl.BlockSpec((1, L, width), lambda b, t: (b, t, 1)),
            pl.BlockSpec((1, 2 * ML_HEADS, L), lambda b, t: (b, 0, t)),
            pl.BlockSpec((ML_CONV, width), vec),
            pl.BlockSpec((1, width), vec),
            pl.BlockSpec((ML_HEADS, dh, dh), lambda b, t: (0, 0, 0)),
            pl.BlockSpec((ML_HEADS, dh, dh), lambda b, t: (0, 0, 0)),
            pl.BlockSpec((1, width), vec),
            pl.BlockSpec((1, width), vec),
        ],
        out_specs=pl.BlockSpec((1, L, width), lambda b, t: (b, t, 0)),
        scratch_shapes=[
            pltpu.VMEM((ML_HEADS, dh, dh), F32),
            pltpu.VMEM((ML_HEADS, dh), F32),
            pltpu.VMEM((ML_HEADS, LANES), F32),
            pltpu.VMEM((8, width), F32),
        ],
        compiler_params=_cparams(("arbitrary", "arbitrary")),
        name="mlstm",
    )(bias, z3, z3, gates, conv_w, conv_b.reshape(1, width), wq.astype(BF16),
      wk.astype(BF16), norm_w.reshape(1, width), skip.reshape(1, width))


def _midrow_broadcast(b, m):
    n, w = b.shape
    row = lambda i: b[i:i + 1]
    if m >= 8:
        return jnp.concatenate(
            [jnp.broadcast_to(row(p * 2 * m + m), (2 * m, w)) for p in range(n // (2 * m))], axis=0)
    pick = lambda off: jnp.concatenate(
        [jnp.broadcast_to(row(8 * g + off), (8, w)) for g in range(n // 8)], axis=0)
    sub = lax.broadcasted_iota(jnp.int32, (n, w), 0) % 8
    if m == 4:
        return pick(4)
    if m == 2:
        return jnp.where(sub < 4, pick(2), pick(6))
    return jnp.where(sub < 2, pick(1), jnp.where(sub < 4, pick(3), jnp.where(sub < 6, pick(5), pick(7))))


def _hgrn_kernel(q_ref, f_ref, i_ref, g_ref, lb_ref, nw_ref, y_ref, s_ref):
    C, W = q_ref.shape[1], q_ref.shape[2]
    H = W // HG_EXPAND

    @pl.when(pl.program_id(1) == 0)
    def _():
        s_ref[...] = jnp.zeros_like(s_ref)

    lb = lb_ref[...]
    sig = _sigmoid(f_ref[0])
    q = _silu(q_ref[0])
    k = (1.0 - lb) * (1.0 - sig)
    log_f = jnp.log(jnp.maximum(lb + (1.0 - lb) * sig, TINY))
    r = lax.broadcasted_iota(jnp.int32, (C, C), 0)
    c = lax.broadcasted_iota(jnp.int32, (C, C), 1)
    b = _cumsum_rows((r >= c).astype(BF16), log_f)
    v = i_ref[0]
    vb = v.astype(BF16)

    rowi = lax.broadcasted_iota(jnp.int32, (C, W), 0)
    zs, masks = [], []
    m = 1
    while m < C:
        e = jnp.exp(-jnp.abs(b - _midrow_broadcast(b, m)))
        zs.append((jnp.where((rowi % (2 * m)) >= m, q, k) * e).astype(BF16))
        masks.append(((r // (2 * m)) == (c // (2 * m))) & ((r % (2 * m)) >= m) & ((c % (2 * m)) < m))
        m *= 2

    qe = (q * jnp.exp(b)).astype(BF16)
    b_end = b[C - 1:C]
    kd = (k * jnp.exp(b_end - b)).astype(BF16)
    g_end = jnp.exp(b_end)
    diag = q * k
    gate = _gate_sigmoid(g_ref[0])
    nw = nw_ref[...]
    heads = range(H)
    sls = [slice(h * HG_EXPAND, (h + 1) * HG_EXPAND) for h in heads]
    st = [s_ref[h] for h in heads]
    inter = [_dot_t(qe[:, sl], st[h].astype(BF16)) for h, sl in zip(heads, sls)]
    for h, sl in zip(heads, sls):
        s_ref[h] = st[h] * g_end[:, sl] + _tdot(vb[:, sl], kd[:, sl])
    a = [jnp.zeros((C, C), F32)] * H
    for z_l, mask in zip(zs, masks):
        for h, sl in zip(heads, sls):
            a[h] = jnp.where(mask, _dot_t(z_l[:, sl], z_l[:, sl]), a[h])
    for h, sl in zip(heads, sls):
        o = (inter[h] + _dot(a[h].astype(BF16), vb[:, sl])
             + jnp.sum(diag[:, sl], axis=-1, keepdims=True) * v[:, sl])
        on = o * lax.rsqrt(jnp.mean(o * o, axis=-1, keepdims=True) + EPS)
        y_ref[0, :, sl] = on * nw[:, sl] * gate[:, sl]


def _hgrn(z3, lower_bound, norm_w, *, col0, width):
    B, T, _ = z3.shape
    C = min(HG_CHUNK, T)
    H = width // HG_EXPAND
    blk0 = col0 // width
    spec = lambda j: pl.BlockSpec((1, C, width), lambda b, t: (b, t, blk0 + j))
    vec = pl.BlockSpec((1, width), lambda b, t: (0, 0))
    return pl.pallas_call(
        _hgrn_kernel,
        out_shape=jax.ShapeDtypeStruct((B, T, width), F32),
        grid=(B, T // C),
        in_specs=[spec(0), spec(1), spec(2), spec(3), vec, vec],
        out_specs=pl.BlockSpec((1, C, width), lambda b, t: (b, t, 0)),
        scratch_shapes=[pltpu.VMEM((H, HG_EXPAND, HG_EXPAND), F32)],
        compiler_params=_cparams(("arbitrary", "arbitrary")),
        name="hgrn2",
    )(z3, z3, z3, z3, lower_bound.reshape(1, width), norm_w.reshape(1, width))


def _token_shift(z, mu, last_ref):
    n = z.shape[0]
    row = lax.broadcasted_iota(jnp.int32, z.shape, 0)
    prev = jnp.where(row == 0, last_ref[7:8], pltpu.roll(z, 1, 0))
    last_ref[...] = z[n - 8:n]
    return z + mu * (prev - z)


def _head_sum(x, ones_bd, two_pass=False):
    n, w = x.shape
    slab = ones_bd.shape[0]
    xs = jnp.concatenate([x[:, g * slab:(g + 1) * slab] for g in range(w // slab)], axis=0)
    hi = xs.astype(BF16)
    s = _dot(hi, ones_bd)
    if two_pass:
        s = s + _dot((xs - hi.astype(F32)).astype(BF16), ones_bd)
    return jnp.concatenate([s[g * n:(g + 1) * n] for g in range(w // slab)], axis=1)


def _softplus(x):
    return jnp.maximum(x, 0.0) + jnp.log1p(jnp.exp(-jnp.abs(x)))


def _rwkv_kernel(*refs, has_vres):
    it = iter(refs)
    r_ref, k_ref, v_ref, lo_ref = next(it), next(it), next(it), next(it)
    sm_ref, vf_ref = (next(it), next(it)) if has_vres else (None, None)
    mu_ref, mulo_ref, vec_ref, w2_ref, a2_ref, g2_ref = (next(it) for _ in range(6))
    musm_ref, v2_ref = (next(it), next(it)) if has_vres else (None, None)
    y_ref = next(it)
    vfo_ref = None if has_vres else next(it)
    s_ref, pr_ref, pk_ref, pv_ref, plo_ref = (next(it) for _ in range(5))
    psm_ref = next(it) if has_vres else None

    C, W = r_ref.shape[1], r_ref.shape[2]
    P = W // LANES

    @pl.when(pl.program_id(1) == 0)
    def _():
        s_ref[...] = jnp.zeros_like(s_ref)
        for ref in (pr_ref, pk_ref, pv_ref, plo_ref, psm_ref):
            if ref is not None:
                ref[...] = jnp.zeros_like(ref)

    vec = vec_ref[...]
    w0, a0, k_k, k_a, ln_w, ln_b, r_k, v0 = (vec[i:i + 1] for i in range(8))
    mu = mu_ref[...]
    r = _token_shift(r_ref[0], mu[0:1], pr_ref)
    k = _token_shift(k_ref[0], mu[1:2], pk_ref)
    v = _token_shift(v_ref[0], mu[2:3], pv_ref)
    lo = _token_shift(lo_ref[0], mulo_ref[...], plo_ref)
    lo_wa, lo_g = lo[:, :LANES], lo[:, LANES:]

    w_raw = -_softplus(-(w0 + _dot(jnp.tanh(lo_wa).astype(BF16), w2_ref[...]))) - 0.5
    lw = -jnp.exp(w_raw)
    a = _gate_sigmoid(a0 + _dot(lo_wa.astype(BF16), a2_ref[...]))
    g = _dot(_gate_sigmoid(lo_g).astype(BF16), g2_ref[...])
    if has_vres:
        sm = _token_shift(sm_ref[0], musm_ref[...], psm_ref)
        v = v + (vf_ref[0] - v) * _gate_sigmoid(v0 + _dot(sm.astype(BF16), v2_ref[...]))
    else:
        vfo_ref[0] = v

    rr2 = lax.broadcasted_iota(jnp.int32, (2 * LANES, 2 * LANES), 0)
    cc2 = lax.broadcasted_iota(jnp.int32, (2 * LANES, 2 * LANES), 1)
    ones_bd = ((rr2 // RW_HEAD) == (cc2 // RW_HEAD)).astype(BF16)
    rr = lax.broadcasted_iota(jnp.int32, (LANES, LANES), 0)
    cc = lax.broadcasted_iota(jnp.int32, (LANES, LANES), 1)

    kk = k * k_k
    kmod = k * (1.0 + (a - 1.0) * k_a)
    stats = _head_sum(jnp.concatenate([kk * kk, r * kmod * r_k], axis=0), ones_bd)
    kk = kk / jnp.maximum(jnp.sqrt(stats[:C]), 1e-12)
    bonus_dot = stats[C:]
    ahat = kk * a

    tr = lax.broadcasted_iota(jnp.int32, (C, C), 0)
    tc = lax.broadcasted_iota(jnp.int32, (C, C), 1)
    cum = _cumsum_rows((tr >= tc).astype(BF16), lw)
    e_inc = jnp.exp(cum)
    e_inv = jnp.exp(-cum)
    r_dec = r * e_inc
    kk_dec = kk * jnp.exp(cum - lw)
    a_inv = ahat * e_inv
    k_inv = kmod * e_inv
    gamma_end = e_inc[C - 1:C]

    lane = lax.broadcasted_iota(jnp.int32, (C, LANES), 1)
    first = lane < RW_HEAD

    def stack(zp):
        return jnp.concatenate([jnp.where(first, zp, 0.0), jnp.where(first, 0.0, zp)], axis=0).astype(BF16)

    strict = rr > cc
    incl = rr >= cc
    eye = (rr == cc).astype(F32)
    n_sq = int(np.log2(C)) - 1
    pairs = range(P)
    n2 = 2 * C
    sls = [slice(p * LANES, (p + 1) * LANES) for p in pairs]
    kks = [stack(kk_dec[:, sl]) for sl in sls]
    rs = [stack(r_dec[:, sl]) for sl in sls]
    vs = [stack(v[:, sl]) for sl in sls]
    ak = [jnp.concatenate([stack(a_inv[:, sl]), stack(k_inv[:, sl])], axis=0) for sl in sls]
    sc = [_dot_t(jnp.concatenate([kks[p], rs[p]], axis=0), ak[p]) for p in pairs]
    l_a = [jnp.where(strict, sc[p][:n2, :n2], 0.0) for p in pairs]
    lkv = [_dot(jnp.where(strict, sc[p][:n2, n2:], 0.0).astype(BF16), vs[p]) for p in pairs]
    a_cat = [jnp.where(jnp.concatenate([incl, incl], axis=1), sc[p][n2:], 0.0).astype(BF16) for p in pairs]
    t_inv = [eye - l_a[p] for p in pairs]
    pw = [_dot(l_a[p].astype(BF16), l_a[p].astype(BF16)).astype(BF16) for p in pairs]
    for _ in range(n_sq - 1):
        both = [_dot(pw[p], jnp.concatenate([t_inv[p].astype(BF16), pw[p]], axis=1)) for p in pairs]
        t_inv = [t_inv[p] + both[p][:, :n2] for p in pairs]
        pw = [both[p][:, n2:].astype(BF16) for p in pairs]
    t_inv = [t_inv[p] + _dot(pw[p], t_inv[p].astype(BF16)) for p in pairs]
    rhs = [jnp.concatenate([-kks[p].astype(F32), -lkv[p]], axis=1).astype(BF16) for p in pairs]
    wv = [_dot(t_inv[p].astype(BF16), rhs[p]) for p in pairs]
    st = [s_ref[p] for p in pairs]
    wr = [_dot_t(jnp.concatenate([wv[p][:, :LANES].astype(BF16), rs[p]], axis=0), st[p].astype(BF16))
          for p in pairs]
    uv = [jnp.concatenate([(wr[p][:n2] + wv[p][:, LANES:]).astype(BF16), vs[p]], axis=0) for p in pairs]
    yst = [wr[p][n2:] + _dot(a_cat[p], uv[p]) for p in pairs]
    for p in pairs:
        s_ref[p] = (st[p] + _tdot(uv[p], ak[p])) * gamma_end[:, sls[p]]
    y = jnp.concatenate([yst[p][:C] + yst[p][C:] for p in pairs], axis=1)

    inv_n = 1.0 / RW_HEAD
    yc = y - _head_sum(y, ones_bd, two_pass=True) * inv_n
    yn = yc * lax.rsqrt(_head_sum(yc * yc, ones_bd) * inv_n + RW_LN_EPS) * ln_w + ln_b
    y_ref[0] = (yn + bonus_dot * v) * g


def _rwkv(z3, v_first, prm, *, rkv_col0, lora_col0, small_col0, width):
    B, T, _ = z3.shape
    C = min(RW_CHUNK, T)
    has_vres = v_first is not None
    lora_w = RW_DECAY_LORA + RW_AAA_LORA + RW_GATE_LORA
    tok = lambda wdt, col0, j=0: pl.BlockSpec((1, C, wdt), lambda b, t: (b, t, col0 // wdt + j))
    full = lambda arr: pl.BlockSpec(arr.shape, lambda b, t: (0,) * arr.ndim)
    tok_out = pl.BlockSpec((1, C, width), lambda b, t: (b, t, 0))

    args = [z3, z3, z3, z3]
    specs = [tok(width, rkv_col0, 0), tok(width, rkv_col0, 1), tok(width, rkv_col0, 2), tok(lora_w, lora_col0)]
    if has_vres:
        args += [z3, v_first]
        specs += [tok(LANES, small_col0), tok_out]
    weights = [prm["mu_rkv"], prm["mu_lora"], prm["vecs"], prm["w2"], prm["a2"], prm["g2"]]
    if has_vres:
        weights += [prm["mu_small"], prm["v2"]]
    args += weights
    specs += [full(wt) for wt in weights]

    y_shape = jax.ShapeDtypeStruct((B, T, width), F32)
    scratch = [pltpu.VMEM((width // LANES, LANES, LANES), F32)]
    scratch += [pltpu.VMEM((8, width), F32)] * 3 + [pltpu.VMEM((8, lora_w), F32)]
    if has_vres:
        scratch += [pltpu.VMEM((8, LANES), F32)]
    out = pl.pallas_call(
        functools.partial(_rwkv_kernel, has_vres=has_vres),
        out_shape=y_shape if has_vres else (y_shape, y_shape),
        grid=(B, T // C),
        in_specs=specs,
        out_specs=tok_out if has_vres else (tok_out, tok_out),
        scratch_shapes=scratch,
        compiler_params=_cparams(("arbitrary", "arbitrary")),
        name="rwkv7",
    )(*args)
    return (out, v_first) if has_vres else out


def _pad_rows(w, row0, rows):
    return jnp.zeros((rows, w.shape[1]), w.dtype).at[row0:row0 + w.shape[0]].set(w)


def kernel(x, c, norm_w, final_norm_w, ada_w, ada_b, ffn_up, ffn_down, w_in, w_in_vres,
           ml_conv_w, ml_conv_b, ml_wq, ml_wk, ml_i_b, ml_f_b, ml_norm_w, ml_skip,
           rw_mu, rw_mu_vres, rw_w0, rw_w2, rw_a0, rw_a2, rw_v0, rw_v2, rw_g2,
           rw_k_k, rw_k_a, rw_r_k, rw_ln_w, rw_ln_b, hg_lb_logits, hg_norm_w,
           w_branch, w_out):
    B, T, D = x.shape
    depth = norm_w.shape[0]
    N = B * T
    W = D
    lora_w = RW_DECAY_LORA + RW_AAA_LORA + RW_GATE_LORA

    o_mlx, o_mlo, o_mli, o_mlf = 0, W, 2 * W, 2 * W + ML_HEADS
    o_rw = 2 * W + 2 * ML_HEADS
    o_hg = o_rw + 3 * W + lora_w
    o_gate = o_hg + 4 * W
    n_in = o_gate + N_BRANCH * D
    c_rkv, c_hg = 2 * W, 5 * W
    c_lora = c_hg + 4 * W
    c_small = c_lora + lora_w
    n_cols = c_small + LANES
    s_vlo = 2 * ML_HEADS

    lb_soft = jax.nn.softmax(hg_lb_logits.astype(F32), axis=0)
    lower_bounds = jnp.cumsum(lb_soft, axis=0) - lb_soft[0]

    mod_all = _ada_mod(c, ada_w, ada_b).reshape(depth, B, N_SUB * 3, D)
    x2 = x.reshape(N, D)
    v_first = None
    for l in range(depth):
        mod = mod_all[l]
        wl = w_in[l]
        small = jnp.concatenate([wl[:, o_mli:o_mli + 2 * ML_HEADS]]
                                + ([w_in_vres[l - 1]] if l > 0 else []), axis=1)
        small = jnp.pad(small, ((0, 0), (0, LANES - small.shape[1])))
        w_cat = jnp.concatenate([
            wl[:, o_mlx:o_mlx + 2 * W], wl[:, o_rw:o_rw + 3 * W], wl[:, o_hg:o_hg + 4 * W],
            wl[:, o_rw + 3 * W:o_rw + 3 * W + lora_w], small], axis=1).astype(BF16)

        x2 = _ffn(x2, mod, norm_w[l, 0], ffn_up[l, 0], ffn_down[l, 0], final_norm_w,
                  sub=0, tokens_per_batch=T, final_norm=False)

        z = _inproj(x2, mod, norm_w[l, 1], w_cat, sub=1, tokens_per_batch=T)
        z3 = z.reshape(B, T, n_cols)

        gates = z3[:, :, c_small:c_small + 2 * ML_HEADS].transpose(0, 2, 1)
        y_ml = _mlstm(z3, gates, jnp.concatenate([ml_i_b[l], ml_f_b[l]]), ml_conv_w[l], ml_conv_b[l],
                      ml_wq[l], ml_wk[l], ml_norm_w[l], ml_skip[l], width=W)

        mu = rw_mu[l]
        prm = {
            "mu_rkv": mu[:3 * W].reshape(3, W),
            "mu_lora": mu[3 * W:].reshape(1, lora_w),
            "vecs": jnp.stack([rw_w0[l], rw_a0[l], rw_k_k[l], rw_k_a[l], rw_ln_w[l], rw_ln_b[l],
                               rw_r_k[l].reshape(W), rw_v0[l - 1] if l > 0 else jnp.zeros((W,), F32)]),
            "w2": _pad_rows(rw_w2[l], 0, LANES).astype(BF16),
            "a2": _pad_rows(rw_a2[l], RW_DECAY_LORA, LANES).astype(BF16),
            "g2": rw_g2[l].astype(BF16),
        }
        if l > 0:
            prm["mu_small"] = jnp.zeros((1, LANES), F32).at[0, s_vlo:s_vlo + RW_MV_LORA].set(rw_mu_vres[l - 1])
            prm["v2"] = _pad_rows(rw_v2[l - 1], s_vlo, LANES).astype(BF16)
        y_rw, v_first = _rwkv(z3, v_first, prm, rkv_col0=c_rkv, lora_col0=c_lora,
                              small_col0=c_small, width=W)

        y_hg = _hgrn(z3, lower_bounds[l], hg_norm_w[l], col0=c_hg, width=W)

        x2 = _branch_mix(x2, mod, norm_w[l, 1], y_ml.reshape(N, W), y_rw.reshape(N, W), y_hg.reshape(N, W),
                         wl[:, o_gate:n_in], w_branch[l], w_out[l], sub=1, tokens_per_batch=T)

        x2 = _ffn(x2, mod, norm_w[l, 2], ffn_up[l, 1], ffn_down[l, 1], final_norm_w,
                  sub=2, tokens_per_batch=T, final_norm=(l == depth - 1))
    return x2.reshape(B, T, D)
```

```python
import functools

import numpy as np
import jax
import jax.numpy as jnp
from jax import lax
from jax.experimental import pallas as pl
from jax.experimental.pallas import tpu as pltpu

F32 = jnp.float32
BF16 = jnp.bfloat16

EPS = 1e-6
NEG_BIG = -1e30
TINY = 1e-30

N_SUB = 3
N_BRANCH = 3
ML_HEADS = 4
ML_CONV = 4
RW_HEAD = 64
RW_LN_EPS = 64e-5
RW_DECAY_LORA = 64
RW_AAA_LORA = 64
RW_MV_LORA = 32
RW_GATE_LORA = 128
HG_EXPAND = 128

LANES = 128
VMEM_LIMIT = 56 * 1024 * 1024

FFN_TM = 512
FFN_FC = 256
INP_TM = 1024
INP_TN = 1920
MIX_TM = 512
ML_CHUNK = 256
RW_CHUNK = 64
RW_BLOCK = 128
HG_CHUNK = 128


def _cparams(sem):
    return pltpu.CompilerParams(dimension_semantics=sem, vmem_limit_bytes=VMEM_LIMIT)


def _dot(a, b):
    return jnp.dot(a, b, preferred_element_type=F32)


def _dot_t(a, b):
    return lax.dot_general(a, b, (((1,), (1,)), ((), ())), preferred_element_type=F32)


def _tdot(a, b):
    return lax.dot_general(a, b, (((0,), (0,)), ((), ())), preferred_element_type=F32)


def _split3(x):
    hi = x.astype(BF16)
    r1 = x - hi.astype(F32)
    mid = r1.astype(BF16)
    lo = (r1 - mid.astype(F32)).astype(BF16)
    return hi, mid, lo


def _cumsum_rows(tri, x):
    hi, mid, lo = _split3(x)
    return _dot(tri, hi) + (_dot(tri, mid) + _dot(tri, lo))


def _cumsum_lanes(x, tri):
    hi, mid, lo = _split3(x)
    return _dot(hi, tri) + (_dot(mid, tri) + _dot(lo, tri))


def _sigmoid(x):
    return 1.0 / (1.0 + jnp.exp(-x))


def _gate_sigmoid(x):
    return 0.5 * jnp.tanh(0.5 * x) + 0.5


def _silu(x):
    return x * _gate_sigmoid(x)


def _modulated_norm(x, nw, shift, scale):
    ms = jnp.mean(x * x, axis=-1, keepdims=True)
    return (x * lax.rsqrt(ms + EPS) * nw) * (1.0 + scale) + shift


def _ada_kernel(c_ref, w_ref, b_ref, o_ref):
    cond = _silu(c_ref[...])
    o_ref[0] = _dot(cond, w_ref[0]) + b_ref[0]


def _ada_mod(c, ada_w, ada_b):
    L, D, M = ada_w.shape
    B = c.shape[0]
    tn = M // 8
    return pl.pallas_call(
        _ada_kernel,
        out_shape=jax.ShapeDtypeStruct((L, B, M), F32),
        grid=(L, M // tn),
        in_specs=[
            pl.BlockSpec((B, D), lambda l, j: (0, 0)),
            pl.BlockSpec((1, D, tn), lambda l, j: (l, 0, j)),
            pl.BlockSpec((1, 1, tn), lambda l, j: (l, 0, j)),
        ],
        out_specs=pl.BlockSpec((1, B, tn), lambda l, j: (l, 0, j)),
        compiler_params=_cparams(("arbitrary", "arbitrary")),
        name="ada_mod",
    )(c, ada_w, ada_b.reshape(L, 1, M))


def _ffn_kernel(x_ref, mod_ref, nw_ref, wa_ref, wb_ref, wd_ref, fw_ref, o_ref, acc_ref,
                *, sub, final_norm):
    x = x_ref[...]
    mod = mod_ref[0]
    shift, scale, gate = (mod[3 * sub + i:3 * sub + i + 1] for i in range(3))
    hb = _modulated_norm(x, nw_ref[...], shift, scale).astype(BF16)
    for c in range(wa_ref.shape[0]):
        a = _dot(hb, wa_ref[c])
        b = _dot(hb, wb_ref[c])
        act = (_silu(a) * b).astype(BF16)
        d = _dot(act, wd_ref[c])
        if c == 0:
            acc_ref[...] = d
        else:
            acc_ref[...] += d
    y = x + (0.5 * (1.0 + gate)) * acc_ref[...]
    if final_norm:
        ms = jnp.mean(y * y, axis=-1, keepdims=True)
        y = y * lax.rsqrt(ms + EPS) * fw_ref[...]
    o_ref[...] = y


def _ffn(x, mod, nw, w_up, w_down, fw, *, sub, tokens_per_batch, final_norm):
    N, D = x.shape
    F = w_down.shape[0]
    nc = F // FFN_FC
    wa = w_up[:, :F].astype(BF16).reshape(D, nc, FFN_FC).transpose(1, 0, 2)
    wb = w_up[:, F:].astype(BF16).reshape(D, nc, FFN_FC).transpose(1, 0, 2)
    wd = w_down.astype(BF16).reshape(nc, FFN_FC, D)
    bpb = tokens_per_batch // FFN_TM
    const3 = lambda i: (0, 0, 0)
    return pl.pallas_call(
        functools.partial(_ffn_kernel, sub=sub, final_norm=final_norm),
        out_shape=jax.ShapeDtypeStruct((N, D), F32),
        grid=(N // FFN_TM,),
        in_specs=[
            pl.BlockSpec((FFN_TM, D), lambda i: (i, 0)),
            pl.BlockSpec((1,) + mod.shape[1:], lambda i: (i // bpb, 0, 0)),
            pl.BlockSpec((1, D), lambda i: (0, 0)),
            pl.BlockSpec(wa.shape, const3, pipeline_mode=pl.Buffered(1)),
            pl.BlockSpec(wb.shape, const3, pipeline_mode=pl.Buffered(1)),
            pl.BlockSpec(wd.shape, const3, pipeline_mode=pl.Buffered(1)),
            pl.BlockSpec((1, D), lambda i: (0, 0)),
        ],
        out_specs=pl.BlockSpec((FFN_TM, D), lambda i: (i, 0)),
        scratch_shapes=[pltpu.VMEM((FFN_TM, D), F32)],
        compiler_params=_cparams(("arbitrary",)),
        name="ffn_half",
    )(x, mod, nw.reshape(1, D), wa, wb, wd, fw.reshape(1, D))


def _inproj_kernel(x_ref, mod_ref, nw_ref, w_ref, o_ref, h_ref, *, sub):
    @pl.when(pl.program_id(1) == 0)
    def _():
        mod = mod_ref[0]
        shift, scale = mod[3 * sub:3 * sub + 1], mod[3 * sub + 1:3 * sub + 2]
        h_ref[...] = _modulated_norm(x_ref[...], nw_ref[...], shift, scale).astype(BF16)

    o_ref[...] = _dot(h_ref[...], w_ref[...])


def _inproj(x, mod, nw, w, *, sub, tokens_per_batch):
    N, D = x.shape
    M = w.shape[1]
    tm = min(INP_TM, tokens_per_batch)
    bpb = tokens_per_batch // tm
    return pl.pallas_call(
        functools.partial(_inproj_kernel, sub=sub),
        out_shape=jax.ShapeDtypeStruct((N, M), F32),
        grid=(N // tm, M // INP_TN),
        in_specs=[
            pl.BlockSpec((tm, D), lambda i, j: (i, 0)),
            pl.BlockSpec((1,) + mod.shape[1:], lambda i, j: (i // bpb, 0, 0)),
            pl.BlockSpec((1, D), lambda i, j: (0, 0)),
            pl.BlockSpec((D, INP_TN), lambda i, j: (0, j)),
        ],
        out_specs=pl.BlockSpec((tm, INP_TN), lambda i, j: (i, j)),
        scratch_shapes=[pltpu.VMEM((tm, D), BF16)],
        compiler_params=_cparams(("arbitrary", "arbitrary")),
        name="mixer_inproj",
    )(x, mod, nw.reshape(1, D), w)


def _mix_kernel(x_ref, mod_ref, nw_ref, yml_ref, yrw_ref, yhg_ref, wg_ref, wbr_ref, wout_ref, o_ref,
                *, sub):
    x = x_ref[...]
    mod = mod_ref[0]
    shift, scale, gate = (mod[3 * sub + i:3 * sub + i + 1] for i in range(3))
    hb = _modulated_norm(x, nw_ref[...], shift, scale).astype(BF16)
    mixed = None
    for n, y_ref in enumerate((yml_ref, yrw_ref, yhg_ref)):
        term = _gate_sigmoid(_dot(hb, wg_ref[n])) * _dot(y_ref[...].astype(BF16), wbr_ref[n])
        mixed = term if mixed is None else mixed + term
    o_ref[...] = x + (1.0 + gate) * _dot(mixed.astype(BF16), wout_ref[...])


def _branch_mix(x, mod, nw, y_ml, y_rw, y_hg, w_gate, w_branch, w_out, *, sub, tokens_per_batch):
    N, D = x.shape
    bpb = tokens_per_batch // MIX_TM
    tok = pl.BlockSpec((MIX_TM, D), lambda i: (i, 0))
    resident = lambda w: pl.BlockSpec(w.shape, lambda i: (0,) * w.ndim, pipeline_mode=pl.Buffered(1))
    wg = w_gate.astype(BF16).reshape(D, N_BRANCH, D).transpose(1, 0, 2)
    return pl.pallas_call(
        functools.partial(_mix_kernel, sub=sub),
        out_shape=jax.ShapeDtypeStruct((N, D), F32),
        grid=(N // MIX_TM,),
        in_specs=[
            tok,
            pl.BlockSpec((1,) + mod.shape[1:], lambda i: (i // bpb, 0, 0)),
            pl.BlockSpec((1, D), lambda i: (0, 0)),
            tok, tok, tok,
            resident(wg), resident(w_branch), resident(w_out),
        ],
        out_specs=tok,
        compiler_params=_cparams(("arbitrary",)),
        name="branch_mix",
    )(x, mod, nw.reshape(1, D), y_ml, y_rw, y_hg, wg, w_branch.astype(BF16), w_out.astype(BF16))


def _log_sigmoid(x):
    return jnp.minimum(x, 0.0) - jnp.log1p(jnp.exp(-jnp.abs(x)))


def _mlstm_kernel(bias_ref, x_ref, op_ref, g_ref, cw_ref, cb_ref, wq_ref, wk_ref,
                  nw_ref, sk_ref, y_ref, c_ref, m_ref, tail_ref):
    L, W = x_ref.shape[1], x_ref.shape[2]
    H = ML_HEADS
    dh = W // H
    heads = range(H)
    sls = [slice(h * dh, (h + 1) * dh) for h in heads]

    @pl.when(pl.program_id(1) == 0)
    def _():
        c_ref[...] = jnp.zeros_like(c_ref)
        m_ref[...] = jnp.zeros_like(m_ref)
        tail_ref[...] = jnp.zeros_like(tail_ref)

    x = x_ref[0]
    cw = cw_ref[...]
    tail = tail_ref[...]
    row8 = lax.broadcasted_iota(jnp.int32, (8, W), 0)
    conv = x * cw[ML_CONV - 1:ML_CONV]
    for s in range(1, ML_CONV):
        xr = pltpu.roll(x, s, 0)
        top = jnp.where(row8 < s, pltpu.roll(tail, s, 0), xr[0:8])
        xs = jnp.concatenate([top, xr[8:]], axis=0)
        conv = conv + xs * cw[ML_CONV - 1 - s:ML_CONV - s]
    tail_ref[...] = x[L - 8:L]
    xc = _silu(conv + cb_ref[...])
    xcb = xc.astype(BF16)
    vb = x.astype(BF16)

    g = g_ref[0]
    li = jnp.concatenate([g[h:h + 1] + bias_ref[h] for h in heads], axis=0)
    lf = _log_sigmoid(jnp.concatenate([g[H + h:H + h + 1] + bias_ref[H + h] for h in heads], axis=0))
    r = lax.broadcasted_iota(jnp.int32, (L, L), 0)
    c = lax.broadcasted_iota(jnp.int32, (L, L), 1)
    causal = r >= c
    b_rows = _cumsum_lanes(lf, (r <= c).astype(BF16))

    g_rows = li - b_rows
    lane = lax.broadcasted_iota(jnp.int32, (H, L), 1)
    cm = g_rows
    sh = 1
    while sh < L:
        cm = jnp.maximum(cm, jnp.where(lane >= sh, pltpu.roll(cm, sh, 1), NEG_BIG))
        sh *= 2
    m_prev = m_ref[...][:, 0:1]
    mx = jnp.maximum(m_prev, cm)
    w_inter = jnp.exp(m_prev - mx)
    em = jnp.exp(-(b_rows + mx))
    b_end = b_rows[:, L - 1:L]
    log_ws = b_end - b_rows + li
    m_new = jnp.maximum(b_end + m_prev, jnp.max(log_ws, axis=-1, keepdims=True))
    ws = jnp.exp(log_ws - m_new)
    decay = jnp.exp(b_end + m_prev - m_new)
    m_ref[...] = jnp.broadcast_to(m_new, (H, LANES))
    cols = jnp.concatenate([-mx, w_inter, em, ws, jnp.zeros((LANES - 4 * H, L), F32)], axis=0).T

    ones_col = (lax.broadcasted_iota(jnp.int32, (L, LANES), 1) == 0).astype(BF16)
    vaug = [jnp.concatenate([vb[:, sl], ones_col], axis=1) for sl in sls]

    q = [_dot(xcb[:, sl], wq_ref[h]) for h, sl in zip(heads, sls)]
    k = [_dot(xcb[:, sl], wk_ref[h]) * (dh ** -0.5) for h, sl in zip(heads, sls)]
    qb = [t.astype(BF16) for t in q]
    qk = [_dot_t(qb[h], k[h].astype(BF16)) for h in heads]
    c_prev = [c_ref[h] for h in heads]
    qc = [_dot(qb[h], c_prev[h].astype(BF16)) for h in heads]

    hs = []
    for h, sl in zip(heads, sls):
        col = lambda i: cols[:, i * H + h:i * H + h + 1]
        w_intra = jnp.where(causal, jnp.exp(col(0) + g_rows[h:h + 1]), 0.0)
        s_mat = qk[h] * w_intra
        both = col(1) * qc[h] + _dot(s_mat.astype(BF16), vaug[h])
        inv = 1.0 / jnp.maximum(jnp.abs(both[:, dh:dh + 1]), col(2))
        hs.append(both[:, :dh] * inv)
        kw = k[h] * col(3)
        c_ref[h] = decay[h:h + 1] * c_prev[h] + _tdot(kw.astype(BF16), vaug[h])

    gate = _gate_sigmoid(op_ref[0])
    nw, sk = nw_ref[...], sk_ref[...]
    for h, sl in zip(heads, sls):
        hg = hs[h] * gate[:, sl]
        hg = hg - jnp.mean(hg, axis=-1, keepdims=True)
        hn = hg * lax.rsqrt(jnp.mean(hg * hg, axis=-1, keepdims=True) + EPS)
        y_ref[0, :, sl] = hn * nw[:, sl] + sk[:, sl] * xc[:, sl]


def _mlstm(z3, gates, bias, conv_w, conv_b, wq, wk, norm_w, skip, *, width):
    B, T, _ = z3.shape
    dh = width // ML_HEADS
    L = min(ML_CHUNK, T)
    vec = lambda b, t: (0, 0)
    return pl.pallas_call(
        _mlstm_kernel,
        out_shape=jax.ShapeDtypeStruct((B, T, width), F32),
        grid=(B, T // L),
        in_specs=[
            pl.BlockSpec(memory_space=pltpu.SMEM),
            pl.BlockSpec((1, L, width), lambda b, t: (b, t, 0)),
            pl.BlockSpec((1, L, width), lambda b, t: (b, t, 1)),
            pl.BlockSpec((1, 2 * ML_HEADS, L), lambda b, t: (b, 0, t)),
            pl.BlockSpec((ML_CONV, width), vec),
            pl.BlockSpec((1, width), vec),
            pl.BlockSpec((ML_HEADS, dh, dh), lambda b, t: (0, 0, 0)),
            pl.BlockSpec((ML_HEADS, dh, dh), lambda b, t: (0, 0, 0)),
            pl.BlockSpec((1, width), vec),
            pl.BlockSpec((1, width), vec),
        ],
        out_specs=pl.BlockSpec((1, L, width), lambda b, t: (b, t, 0)),
        scratch_shapes=[
            pltpu.VMEM((ML_HEADS, dh, dh + LANES), F32),
            pltpu.VMEM((ML_HEADS, LANES), F32),
            pltpu.VMEM((8, width), F32),
        ],
        compiler_params=_cparams(("arbitrary", "arbitrary")),
        name="mlstm",
    )(bias, z3, z3, gates, conv_w, conv_b.reshape(1, width), wq.astype(BF16),
      wk.astype(BF16), norm_w.reshape(1, width), skip.reshape(1, width))


def _midrow_broadcast(b, m):
    n, w = b.shape
    row = lambda i: b[i:i + 1]
    if m >= 8:
        return jnp.concatenate(
            [jnp.broadcast_to(row(p * 2 * m + m), (2 * m, w)) for p in range(n // (2 * m))], axis=0)
    pick = lambda off: jnp.concatenate(
        [jnp.broadcast_to(row(8 * g + off), (8, w)) for g in range(n // 8)], axis=0)
    sub = lax.broadcasted_iota(jnp.int32, (n, w), 0) % 8
    if m == 4:
        return pick(4)
    if m == 2:
        return jnp.where(sub < 4, pick(2), pick(6))
    return jnp.where(sub < 2, pick(1), jnp.where(sub < 4, pick(3), jnp.where(sub < 6, pick(5), pick(7))))


def _hgrn_kernel(q_ref, f_ref, i_ref, g_ref, lb_ref, nw_ref, y_ref, s_ref):
    C, W = q_ref.shape[1], q_ref.shape[2]
    H = W // HG_EXPAND

    @pl.when(pl.program_id(1) == 0)
    def _():
        s_ref[...] = jnp.zeros_like(s_ref)

    lb = lb_ref[...]
    sig = _sigmoid(f_ref[0])
    q = _silu(q_ref[0])
    k = (1.0 - lb) * (1.0 - sig)
    log_f = jnp.log(jnp.maximum(lb + (1.0 - lb) * sig, TINY))
    r = lax.broadcasted_iota(jnp.int32, (C, C), 0)
    c = lax.broadcasted_iota(jnp.int32, (C, C), 1)
    b = _cumsum_rows((r >= c).astype(BF16), log_f)
    v = i_ref[0]
    vb = v.astype(BF16)

    rowi = lax.broadcasted_iota(jnp.int32, (C, W), 0)
    zs, masks = [], []
    m = 1
    while m < C:
        e = jnp.exp(-jnp.abs(b - _midrow_broadcast(b, m)))
        zs.append((jnp.where((rowi % (2 * m)) >= m, q, k) * e).astype(BF16))
        masks.append(((r // (2 * m)) == (c // (2 * m))) & ((r % (2 * m)) >= m) & ((c % (2 * m)) < m))
        m *= 2

    qe = (q * jnp.exp(b)).astype(BF16)
    b_end = b[C - 1:C]
    kd = (k * jnp.exp(b_end - b)).astype(BF16)
    g_end = jnp.exp(b_end)
    diag = q * k
    gate = _gate_sigmoid(g_ref[0])
    nw = nw_ref[...]
    heads = range(H)
    sls = [slice(h * HG_EXPAND, (h + 1) * HG_EXPAND) for h in heads]
    st = [s_ref[h] for h in heads]
    inter = [_dot_t(qe[:, sl], st[h].astype(BF16)) for h, sl in zip(heads, sls)]
    for h, sl in zip(heads, sls):
        s_ref[h] = st[h] * g_end[:, sl] + _tdot(vb[:, sl], kd[:, sl])
    a = [jnp.zeros((C, C), F32)] * H
    for z_l, mask in zip(zs, masks):
        for h, sl in zip(heads, sls):
            a[h] = jnp.where(mask, _dot_t(z_l[:, sl], z_l[:, sl]), a[h])
    for h, sl in zip(heads, sls):
        o = (inter[h] + _dot(a[h].astype(BF16), vb[:, sl])
             + jnp.sum(diag[:, sl], axis=-1, keepdims=True) * v[:, sl])
        on = o * lax.rsqrt(jnp.mean(o * o, axis=-1, keepdims=True) + EPS)
        y_ref[0, :, sl] = on * nw[:, sl] * gate[:, sl]


def _hgrn(z3, lower_bound, norm_w, *, col0, width):
    B, T, _ = z3.shape
    C = min(HG_CHUNK, T)
    H = width // HG_EXPAND
    blk0 = col0 // width
    spec = lambda j: pl.BlockSpec((1, C, width), lambda b, t: (b, t, blk0 + j))
    vec = pl.BlockSpec((1, width), lambda b, t: (0, 0))
    return pl.pallas_call(
        _hgrn_kernel,
        out_shape=jax.ShapeDtypeStruct((B, T, width), F32),
        grid=(B, T // C),
        in_specs=[spec(0), spec(1), spec(2), spec(3), vec, vec],
        out_specs=pl.BlockSpec((1, C, width), lambda b, t: (b, t, 0)),
        scratch_shapes=[pltpu.VMEM((H, HG_EXPAND, HG_EXPAND), F32)],
        compiler_params=_cparams(("arbitrary", "arbitrary")),
        name="hgrn2",
    )(z3, z3, z3, z3, lower_bound.reshape(1, width), norm_w.reshape(1, width))


def _token_shift(z, mu, last_ref):
    n = z.shape[0]
    row = lax.broadcasted_iota(jnp.int32, z.shape, 0)
    prev = jnp.where(row == 0, last_ref[7:8], pltpu.roll(z, 1, 0))
    last_ref[...] = z[n - 8:n]
    return z + mu * (prev - z)


def _head_sum(x, ones_bd, two_pass=False):
    n, w = x.shape
    slab = ones_bd.shape[0]
    xs = jnp.concatenate([x[:, g * slab:(g + 1) * slab] for g in range(w // slab)], axis=0)
    hi = xs.astype(BF16)
    s = _dot(hi, ones_bd)
    if two_pass:
        s = s + _dot((xs - hi.astype(F32)).astype(BF16), ones_bd)
    return jnp.concatenate([s[g * n:(g + 1) * n] for g in range(w // slab)], axis=1)


def _softplus(x):
    return jnp.maximum(x, 0.0) + jnp.log1p(jnp.exp(-jnp.abs(x)))


def _rwkv_kernel(*refs, has_vres):
    it = iter(refs)
    r_ref, k_ref, v_ref, lo_ref = next(it), next(it), next(it), next(it)
    sm_ref, vf_ref = (next(it), next(it)) if has_vres else (None, None)
    mu_ref, mulo_ref, vec_ref, w2_ref, a2_ref, g2_ref = (next(it) for _ in range(6))
    musm_ref, v2_ref = (next(it), next(it)) if has_vres else (None, None)
    y_ref = next(it)
    vfo_ref = None if has_vres else next(it)
    s_ref, pr_ref, pk_ref, pv_ref, plo_ref = (next(it) for _ in range(5))
    psm_ref = next(it) if has_vres else None

    TB, W = r_ref.shape[1], r_ref.shape[2]
    C = min(RW_CHUNK, TB)
    P = W // LANES

    @pl.when(pl.program_id(1) == 0)
    def _():
        s_ref[...] = jnp.zeros_like(s_ref)
        for ref in (pr_ref, pk_ref, pv_ref, plo_ref, psm_ref):
            if ref is not None:
                ref[...] = jnp.zeros_like(ref)

    vec = vec_ref[...]
    w0, a0, k_k, k_a, ln_w, ln_b, r_k, v0 = (vec[i:i + 1] for i in range(8))
    mu = mu_ref[...]
    r = _token_shift(r_ref[0], mu[0:1], pr_ref)
    k = _token_shift(k_ref[0], mu[1:2], pk_ref)
    v = _token_shift(v_ref[0], mu[2:3], pv_ref)
    lo = _token_shift(lo_ref[0], mulo_ref[...], plo_ref)
    lo_wa, lo_g = lo[:, :LANES], lo[:, LANES:]

    w_raw = -_softplus(-(w0 + _dot(jnp.tanh(lo_wa).astype(BF16), w2_ref[...]))) - 0.5
    lw = -jnp.exp(w_raw)
    a = _gate_sigmoid(a0 + _dot(lo_wa.astype(BF16), a2_ref[...]))
    g = _dot(_gate_sigmoid(lo_g).astype(BF16), g2_ref[...])
    if has_vres:
        sm = _token_shift(sm_ref[0], musm_ref[...], psm_ref)
        v = v + (vf_ref[0] - v) * _gate_sigmoid(v0 + _dot(sm.astype(BF16), v2_ref[...]))
    else:
        vfo_ref[0] = v

    rr2 = lax.broadcasted_iota(jnp.int32, (2 * LANES, 2 * LANES), 0)
    cc2 = lax.broadcasted_iota(jnp.int32, (2 * LANES, 2 * LANES), 1)
    ones_bd = ((rr2 // RW_HEAD) == (cc2 // RW_HEAD)).astype(BF16)
    rr = lax.broadcasted_iota(jnp.int32, (LANES, LANES), 0)
    cc = lax.broadcasted_iota(jnp.int32, (LANES, LANES), 1)

    kk = k * k_k
    kmod = k * (1.0 + (a - 1.0) * k_a)
    stats = _head_sum(jnp.concatenate([kk * kk, r * kmod * r_k], axis=0), ones_bd)
    kk = kk / jnp.maximum(jnp.sqrt(stats[:TB]), 1e-12)
    bonus_dot = stats[TB:]
    ahat = kk * a

    tr = lax.broadcasted_iota(jnp.int32, (TB, TB), 0)
    tc = lax.broadcasted_iota(jnp.int32, (TB, TB), 1)
    cum = _cumsum_rows(((tr >= tc) & ((tr // C) == (tc // C))).astype(BF16), lw)
    e_inc = jnp.exp(cum)
    e_inv = jnp.exp(-cum)
    r_dec = r * e_inc
    kk_dec = kk * jnp.exp(cum - lw)
    a_inv = ahat * e_inv
    k_inv = kmod * e_inv

    lane = lax.broadcasted_iota(jnp.int32, (C, LANES), 1)
    first = lane < RW_HEAD

    def stack(zp):
        return jnp.concatenate([jnp.where(first, zp, 0.0), jnp.where(first, 0.0, zp)], axis=0).astype(BF16)

    strict = rr > cc
    incl = rr >= cc
    eye = (rr == cc).astype(F32)
    n_sq = int(np.log2(C)) - 1
    n2 = 2 * C
    units = [(j, p) for j in range(TB // C) for p in range(P)]
    cut = lambda t, u: t[u[0] * C:(u[0] + 1) * C, u[1] * LANES:(u[1] + 1) * LANES]
    kks = {u: stack(cut(kk_dec, u)) for u in units}
    rs = {u: stack(cut(r_dec, u)) for u in units}
    vs = {u: stack(cut(v, u)) for u in units}
    ak = {u: jnp.concatenate([stack(cut(a_inv, u)), stack(cut(k_inv, u))], axis=0) for u in units}
    sc = {u: _dot_t(jnp.concatenate([kks[u], rs[u]], axis=0), ak[u]) for u in units}
    l_a = {u: jnp.where(strict, sc[u][:n2, :n2], 0.0) for u in units}
    lkv = {u: _dot(jnp.where(strict, sc[u][:n2, n2:], 0.0).astype(BF16), vs[u]) for u in units}
    a_cat = {u: jnp.where(jnp.concatenate([incl, incl], axis=1), sc[u][n2:], 0.0).astype(BF16) for u in units}
    t_inv = {u: eye - l_a[u] for u in units}
    pw = {u: _dot(l_a[u].astype(BF16), l_a[u].astype(BF16)).astype(BF16) for u in units}
    for _ in range(n_sq - 1):
        both = {u: _dot(pw[u], jnp.concatenate([t_inv[u].astype(BF16), pw[u]], axis=1)) for u in units}
        t_inv = {u: t_inv[u] + both[u][:, :n2] for u in units}
        pw = {u: both[u][:, n2:].astype(BF16) for u in units}
    t_inv = {u: t_inv[u] + _dot(pw[u], t_inv[u].astype(BF16)) for u in units}
    rhs = {u: jnp.concatenate([-kks[u].astype(F32), -lkv[u]], axis=1).astype(BF16) for u in units}
    wv = {u: _dot(t_inv[u].astype(BF16), rhs[u]) for u in units}

    st = [s_ref[p] for p in range(P)]
    y_rows = []
    for j in range(TB // C):
        us = [(j, p) for p in range(P)]
        wr = [_dot_t(jnp.concatenate([wv[u][:, :LANES].astype(BF16), rs[u]], axis=0), st[u[1]].astype(BF16))
              for u in us]
        uv = [jnp.concatenate([(wr[p][:n2] + wv[u][:, LANES:]).astype(BF16), vs[u]], axis=0)
              for p, u in enumerate(us)]
        yst = [wr[p][n2:] + _dot(a_cat[u], uv[p]) for p, u in enumerate(us)]
        gamma_end = e_inc[(j + 1) * C - 1:(j + 1) * C]
        st = [(st[p] + _tdot(uv[p], ak[u])) * gamma_end[:, p * LANES:(p + 1) * LANES]
              for p, u in enumerate(us)]
        y_rows.append(jnp.concatenate([yst[p][:C] + yst[p][C:] for p in range(P)], axis=1))
    for p in range(P):
        s_ref[p] = st[p]
    y = jnp.concatenate(y_rows, axis=0)

    inv_n = 1.0 / RW_HEAD
    yc = y - _head_sum(y, ones_bd, two_pass=True) * inv_n
    yn = yc * lax.rsqrt(_head_sum(yc * yc, ones_bd) * inv_n + RW_LN_EPS) * ln_w + ln_b
    y_ref[0] = (yn + bonus_dot * v) * g


def _rwkv(z3, v_first, prm, *, rkv_col0, lora_col0, small_col0, width):
    B, T, _ = z3.shape
    C = min(RW_BLOCK, T)
    has_vres = v_first is not None
    lora_w = RW_DECAY_LORA + RW_AAA_LORA + RW_GATE_LORA
    tok = lambda wdt, col0, j=0: pl.BlockSpec((1, C, wdt), lambda b, t: (b, t, col0 // wdt + j))
    full = lambda arr: pl.BlockSpec(arr.shape, lambda b, t: (0,) * arr.ndim)
    tok_out = pl.BlockSpec((1, C, width), lambda b, t: (b, t, 0))

    args = [z3, z3, z3, z3]
    specs = [tok(width, rkv_col0, 0), tok(width, rkv_col0, 1), tok(width, rkv_col0, 2), tok(lora_w, lora_col0)]
    if has_vres:
        args += [z3, v_first]
        specs += [tok(LANES, small_col0), tok_out]
    weights = [prm["mu_rkv"], prm["mu_lora"], prm["vecs"], prm["w2"], prm["a2"], prm["g2"]]
    if has_vres:
        weights += [prm["mu_small"], prm["v2"]]
    args += weights
    specs += [full(wt) for wt in weights]

    y_shape = jax.ShapeDtypeStruct((B, T, width), F32)
    scratch = [pltpu.VMEM((width // LANES, LANES, LANES), F32)]
    scratch += [pltpu.VMEM((8, width), F32)] * 3 + [pltpu.VMEM((8, lora_w), F32)]
    if has_vres:
        scratch += [pltpu.VMEM((8, LANES), F32)]
    out = pl.pallas_call(
        functools.partial(_rwkv_kernel, has_vres=has_vres),
        out_shape=y_shape if has_vres else (y_shape, y_shape),
        grid=(B, T // C),
        in_specs=specs,
        out_specs=tok_out if has_vres else (tok_out, tok_out),
        scratch_shapes=scratch,
        compiler_params=_cparams(("arbitrary", "arbitrary")),
        name="rwkv7",
    )(*args)
    return (out, v_first) if has_vres else out


def _pad_rows(w, row0, rows):
    return jnp.zeros((rows, w.shape[1]), w.dtype).at[row0:row0 + w.shape[0]].set(w)


def kernel(x, c, norm_w, final_norm_w, ada_w, ada_b, ffn_up, ffn_down, w_in, w_in_vres,
           ml_conv_w, ml_conv_b, ml_wq, ml_wk, ml_i_b, ml_f_b, ml_norm_w, ml_skip,
           rw_mu, rw_mu_vres, rw_w0, rw_w2, rw_a0, rw_a2, rw_v0, rw_v2, rw_g2,
           rw_k_k, rw_k_a, rw_r_k, rw_ln_w, rw_ln_b, hg_lb_logits, hg_norm_w,
           w_branch, w_out):
    B, T, D = x.shape
    depth = norm_w.shape[0]
    N = B * T
    W = D
    lora_w = RW_DECAY_LORA + RW_AAA_LORA + RW_GATE_LORA

    o_mlx, o_mlo, o_mli, o_mlf = 0, W, 2 * W, 2 * W + ML_HEADS
    o_rw = 2 * W + 2 * ML_HEADS
    o_hg = o_rw + 3 * W + lora_w
    o_gate = o_hg + 4 * W
    n_in = o_gate + N_BRANCH * D
    c_rkv, c_hg = 2 * W, 5 * W
    c_lora = c_hg + 4 * W
    c_small = c_lora + lora_w
    n_cols = c_small + LANES
    s_vlo = 2 * ML_HEADS

    lb_soft = jax.nn.softmax(hg_lb_logits.astype(F32), axis=0)
    lower_bounds = jnp.cumsum(lb_soft, axis=0) - lb_soft[0]

    mod_all = _ada_mod(c, ada_w, ada_b).reshape(depth, B, N_SUB * 3, D)
    x2 = x.reshape(N, D)
    v_first = None
    for l in range(depth):
        mod = mod_all[l]
        wl = w_in[l]
        small = jnp.concatenate([wl[:, o_mli:o_mli + 2 * ML_HEADS]]
                                + ([w_in_vres[l - 1]] if l > 0 else []), axis=1)
        small = jnp.pad(small, ((0, 0), (0, LANES - small.shape[1])))
        w_cat = jnp.concatenate([
            wl[:, o_mlx:o_mlx + 2 * W], wl[:, o_rw:o_rw + 3 * W], wl[:, o_hg:o_hg + 4 * W],
            wl[:, o_rw + 3 * W:o_rw + 3 * W + lora_w], small], axis=1).astype(BF16)

        x2 = _ffn(x2, mod, norm_w[l, 0], ffn_up[l, 0], ffn_down[l, 0], final_norm_w,
                  sub=0, tokens_per_batch=T, final_norm=False)

        z = _inproj(x2, mod, norm_w[l, 1], w_cat, sub=1, tokens_per_batch=T)
        z3 = z.reshape(B, T, n_cols)

        gates = z3[:, :, c_small:c_small + 2 * ML_HEADS].transpose(0, 2, 1)
        y_ml = _mlstm(z3, gates, jnp.concatenate([ml_i_b[l], ml_f_b[l]]), ml_conv_w[l], ml_conv_b[l],
                      ml_wq[l], ml_wk[l], ml_norm_w[l], ml_skip[l], width=W)

        mu = rw_mu[l]
        prm = {
            "mu_rkv": mu[:3 * W].reshape(3, W),
            "mu_lora": mu[3 * W:].reshape(1, lora_w),
            "vecs": jnp.stack([rw_w0[l], rw_a0[l], rw_k_k[l], rw_k_a[l], rw_ln_w[l], rw_ln_b[l],
                               rw_r_k[l].reshape(W), rw_v0[l - 1] if l > 0 else jnp.zeros((W,), F32)]),
            "w2": _pad_rows(rw_w2[l], 0, LANES).astype(BF16),
            "a2": _pad_rows(rw_a2[l], RW_DECAY_LORA, LANES).astype(BF16),
            "g2": rw_g2[l].astype(BF16),
        }
        if l > 0:
            prm["mu_small"] = jnp.zeros((1, LANES), F32).at[0, s_vlo:s_vlo + RW_MV_LORA].set(rw_mu_vres[l - 1])
            prm["v2"] = _pad_rows(rw_v2[l - 1], s_vlo, LANES).astype(BF16)
        y_rw, v_first = _rwkv(z3, v_first, prm, rkv_col0=c_rkv, lora_col0=c_lora,
                              small_col0=c_small, width=W)

        y_hg = _hgrn(z3, lower_bounds[l], hg_norm_w[l], col0=c_hg, width=W)

        x2 = _branch_mix(x2, mod, norm_w[l, 1], y_ml.reshape(N, W), y_rw.reshape(N, W), y_hg.reshape(N, W),
                         wl[:, o_gate:n_in], w_branch[l], w_out[l], sub=1, tokens_per_batch=T)

        x2 = _ffn(x2, mod, norm_w[l, 2], ffn_up[l, 1], ffn_down[l, 1], final_norm_w,
                  sub=2, tokens_per_batch=T, final_norm=(l == depth - 1))
    return x2.reshape(B, T, D)
```

```python
import functools

import numpy as np
import jax
import jax.numpy as jnp
from jax import lax
from jax.experimental import pallas as pl
from jax.experimental.pallas import tpu as pltpu

F32 = jnp.float32
BF16 = jnp.bfloat16

EPS = 1e-6
NEG_BIG = -1e30
TINY = 1e-30

N_SUB = 3
N_BRANCH = 3
ML_HEADS = 4
ML_CONV = 4
RW_HEAD = 64
RW_LN_EPS = 64e-5
RW_DECAY_LORA = 64
RW_AAA_LORA = 64
RW_MV_LORA = 32
RW_GATE_LORA = 128
HG_EXPAND = 128

LANES = 128
VMEM_LIMIT = 56 * 1024 * 1024

FFN_TM = 512
FFN_FC = 256
INP_TM = 1024
INP_TN = 1920
MIX_TM = 512
ML_CHUNK = 256
ML_BLOCK = 256
RW_CHUNK = 64
RW_BLOCK = 256
HG_CHUNK = 128


def _cparams(sem):
    return pltpu.CompilerParams(dimension_semantics=sem, vmem_limit_bytes=VMEM_LIMIT)


def _dot(a, b):
    return jnp.dot(a, b, preferred_element_type=F32)


def _dot_t(a, b):
    return lax.dot_general(a, b, (((1,), (1,)), ((), ())), preferred_element_type=F32)


def _tdot(a, b):
    return lax.dot_general(a, b, (((0,), (0,)), ((), ())), preferred_element_type=F32)


def _split3(x):
    hi = x.astype(BF16)
    r1 = x - hi.astype(F32)
    mid = r1.astype(BF16)
    lo = (r1 - mid.astype(F32)).astype(BF16)
    return hi, mid, lo


def _cumsum_rows(tri, x):
    hi, mid, lo = _split3(x)
    return _dot(tri, hi) + (_dot(tri, mid) + _dot(tri, lo))


def _cumsum_lanes(x, tri):
    hi, mid, lo = _split3(x)
    return _dot(hi, tri) + (_dot(mid, tri) + _dot(lo, tri))


def _sigmoid(x):
    return 1.0 / (1.0 + jnp.exp(-x))


def _gate_sigmoid(x):
    return 0.5 * jnp.tanh(0.5 * x) + 0.5


def _silu(x):
    return x * _gate_sigmoid(x)


def _modulated_norm(x, nw, shift, scale):
    ms = jnp.mean(x * x, axis=-1, keepdims=True)
    return (x * lax.rsqrt(ms + EPS) * nw) * (1.0 + scale) + shift


def _ada_kernel(c_ref, w_ref, b_ref, o_ref):
    cond = _silu(c_ref[...])
    o_ref[0] = _dot(cond, w_ref[0]) + b_ref[0]


def _ada_mod(c, ada_w, ada_b):
    L, D, M = ada_w.shape
    B = c.shape[0]
    tn = M // 8
    return pl.pallas_call(
        _ada_kernel,
        out_shape=jax.ShapeDtypeStruct((L, B, M), F32),
        grid=(L, M // tn),
        in_specs=[
            pl.BlockSpec((B, D), lambda l, j: (0, 0)),
            pl.BlockSpec((1, D, tn), lambda l, j: (l, 0, j)),
            pl.BlockSpec((1, 1, tn), lambda l, j: (l, 0, j)),
        ],
        out_specs=pl.BlockSpec((1, B, tn), lambda l, j: (l, 0, j)),
        compiler_params=_cparams(("arbitrary", "arbitrary")),
        name="ada_mod",
    )(c, ada_w, ada_b.reshape(L, 1, M))


def _ffn_kernel(x_ref, mod_ref, nw_ref, wu_ref, wd_ref, fw_ref, o_ref, acc_ref, *, sub, final_norm):
    x = x_ref[...]
    mod = mod_ref[0, 0]
    shift, scale, gate = (mod[3 * sub + i:3 * sub + i + 1] for i in range(3))
    hb = _modulated_norm(x, nw_ref[0], shift, scale).astype(BF16)
    F = wd_ref.shape[2]
    for c in range(F // FFN_FC):
        cols = slice(c * FFN_FC, (c + 1) * FFN_FC)
        a = _dot(hb, wu_ref[0, 0, :, cols])
        b = _dot(hb, wu_ref[0, 0, :, F + c * FFN_FC:F + (c + 1) * FFN_FC])
        act = (_silu(a) * b).astype(BF16)
        d = _dot(act, wd_ref[0, 0, cols, :])
        if c == 0:
            acc_ref[...] = d
        else:
            acc_ref[...] += d
    y = x + (0.5 * (1.0 + gate)) * acc_ref[...]
    if final_norm:
        ms = jnp.mean(y * y, axis=-1, keepdims=True)
        y = y * lax.rsqrt(ms + EPS) * fw_ref[...]
    o_ref[...] = y


def _ffn(x, mod_all, norm_w, w_up, w_down, fw, *, layer, half, tokens_per_batch, final_norm):
    N, D = x.shape
    sub = 2 * half
    bpb = tokens_per_batch // FFN_TM
    return pl.pallas_call(
        functools.partial(_ffn_kernel, sub=sub, final_norm=final_norm),
        out_shape=jax.ShapeDtypeStruct((N, D), F32),
        grid=(N // FFN_TM,),
        in_specs=[
            pl.BlockSpec((FFN_TM, D), lambda i: (i, 0)),
            pl.BlockSpec((1, 1) + mod_all.shape[2:], lambda i: (layer, i // bpb, 0, 0)),
            pl.BlockSpec((1, 1, D), lambda i: (N_SUB * layer + sub, 0, 0)),
            pl.BlockSpec((1, 1) + w_up.shape[2:], lambda i: (layer, half, 0, 0), pipeline_mode=pl.Buffered(1)),
            pl.BlockSpec((1, 1) + w_down.shape[2:], lambda i: (layer, half, 0, 0), pipeline_mode=pl.Buffered(1)),
            pl.BlockSpec((1, D), lambda i: (0, 0)),
        ],
        out_specs=pl.BlockSpec((FFN_TM, D), lambda i: (i, 0)),
        scratch_shapes=[pltpu.VMEM((FFN_TM, D), F32)],
        compiler_params=_cparams(("arbitrary",)),
        name="ffn_half",
    )(x, mod_all, norm_w, w_up, w_down, fw.reshape(1, D))


def _inproj_kernel(x_ref, mod_ref, nw_ref, w_ref, o_ref, h_ref, *, sub):
    @pl.when(pl.program_id(1) == 0)
    def _():
        mod = mod_ref[0, 0]
        shift, scale = mod[3 * sub:3 * sub + 1], mod[3 * sub + 1:3 * sub + 2]
        h_ref[...] = _modulated_norm(x_ref[...], nw_ref[0], shift, scale).astype(BF16)

    o_ref[...] = _dot(h_ref[...], w_ref[...])


def _inproj(x, mod_all, norm_w, w, *, layer, tokens_per_batch):
    N, D = x.shape
    M = w.shape[1]
    sub = 1
    tm = min(INP_TM, tokens_per_batch)
    bpb = tokens_per_batch // tm
    return pl.pallas_call(
        functools.partial(_inproj_kernel, sub=sub),
        out_shape=jax.ShapeDtypeStruct((N, M), F32),
        grid=(N // tm, M // INP_TN),
        in_specs=[
            pl.BlockSpec((tm, D), lambda i, j: (i, 0)),
            pl.BlockSpec((1, 1) + mod_all.shape[2:], lambda i, j: (layer, i // bpb, 0, 0)),
            pl.BlockSpec((1, 1, D), lambda i, j: (N_SUB * layer + sub, 0, 0)),
            pl.BlockSpec((D, INP_TN), lambda i, j: (0, j)),
        ],
        out_specs=pl.BlockSpec((tm, INP_TN), lambda i, j: (i, j)),
        scratch_shapes=[pltpu.VMEM((tm, D), BF16)],
        compiler_params=_cparams(("arbitrary", "arbitrary")),
        name="mixer_inproj",
    )(x, mod_all, norm_w, w)


def _mix_kernel(x_ref, mod_ref, nw_ref, yml_ref, yrw_ref, yhg_ref, wg_ref, wbr_ref, wout_ref, o_ref,
                *, sub):
    x = x_ref[...]
    D = x.shape[1]
    mod = mod_ref[0, 0]
    shift, scale, gate = (mod[3 * sub + i:3 * sub + i + 1] for i in range(3))
    hb = _modulated_norm(x, nw_ref[0], shift, scale).astype(BF16)
    mixed = None
    for n, y_ref in enumerate((yml_ref, yrw_ref, yhg_ref)):
        gate_pre = _dot(hb, wg_ref[:, n * D:(n + 1) * D])
        term = _gate_sigmoid(gate_pre) * _dot(y_ref[...].astype(BF16), wbr_ref[0, n])
        mixed = term if mixed is None else mixed + term
    o_ref[...] = x + (1.0 + gate) * _dot(mixed.astype(BF16), wout_ref[0])


def _branch_mix(x, mod_all, norm_w, y_ml, y_rw, y_hg, w_gate, w_branch, w_out, *, layer, tokens_per_batch):
    N, D = x.shape
    sub = 1
    bpb = tokens_per_batch // MIX_TM
    tok = pl.BlockSpec((MIX_TM, D), lambda i: (i, 0))
    return pl.pallas_call(
        functools.partial(_mix_kernel, sub=sub),
        out_shape=jax.ShapeDtypeStruct((N, D), F32),
        grid=(N // MIX_TM,),
        in_specs=[
            tok,
            pl.BlockSpec((1, 1) + mod_all.shape[2:], lambda i: (layer, i // bpb, 0, 0)),
            pl.BlockSpec((1, 1, D), lambda i: (N_SUB * layer + sub, 0, 0)),
            tok, tok, tok,
            pl.BlockSpec(w_gate.shape, lambda i: (0, 0), pipeline_mode=pl.Buffered(1)),
            pl.BlockSpec((1,) + w_branch.shape[1:], lambda i: (layer, 0, 0, 0), pipeline_mode=pl.Buffered(1)),
            pl.BlockSpec((1,) + w_out.shape[1:], lambda i: (layer, 0, 0), pipeline_mode=pl.Buffered(1)),
        ],
        out_specs=tok,
        compiler_params=_cparams(("arbitrary",)),
        name="branch_mix",
    )(x, mod_all, norm_w, y_ml, y_rw, y_hg, w_gate, w_branch, w_out)


def _log_sigmoid(x):
    return jnp.minimum(x, 0.0) - jnp.log1p(jnp.exp(-jnp.abs(x)))


def _mlstm_kernel(bias_ref, x_ref, op_ref, g_ref, cw_ref, cb_ref, wq_ref, wk_ref,
                  nw_ref, sk_ref, y_ref, c_ref, m_ref, tail_ref):
    TB, W = x_ref.shape[1], x_ref.shape[2]
    L = min(ML_CHUNK, TB)
    H = ML_HEADS
    dh = W // H
    heads = range(H)
    chunks = range(TB // L)
    sls = [slice(h * dh, (h + 1) * dh) for h in heads]
    rws = [slice(j * L, (j + 1) * L) for j in chunks]

    @pl.when(pl.program_id(1) == 0)
    def _():
        c_ref[...] = jnp.zeros_like(c_ref)
        m_ref[...] = jnp.zeros_like(m_ref)
        tail_ref[...] = jnp.zeros_like(tail_ref)

    x = x_ref[0]
    cw = cw_ref[...]
    tail = tail_ref[...]
    row8 = lax.broadcasted_iota(jnp.int32, (8, W), 0)
    conv = x * cw[ML_CONV - 1:ML_CONV]
    for s in range(1, ML_CONV):
        xr = pltpu.roll(x, s, 0)
        top = jnp.where(row8 < s, pltpu.roll(tail, s, 0), xr[0:8])
        xs = jnp.concatenate([top, xr[8:]], axis=0)
        conv = conv + xs * cw[ML_CONV - 1 - s:ML_CONV - s]
    tail_ref[...] = x[TB - 8:TB]
    xc = _silu(conv + cb_ref[...])
    xcb = xc.astype(BF16)
    vb = x.astype(BF16)

    g = g_ref[0]
    li_all = jnp.concatenate([g[h:h + 1] + bias_ref[h] for h in heads], axis=0)
    lf_all = _log_sigmoid(jnp.concatenate([g[H + h:H + h + 1] + bias_ref[H + h] for h in heads], axis=0))
    r = lax.broadcasted_iota(jnp.int32, (L, L), 0)
    c = lax.broadcasted_iota(jnp.int32, (L, L), 1)
    causal = r >= c
    tri = (r <= c).astype(BF16)

    lane = lax.broadcasted_iota(jnp.int32, (H, L), 1)
    m_prev = m_ref[...][:, 0:1]
    g_rows, decay, col_rows = [], [], []
    for rw in rws:
        li = li_all[:, rw]
        b_rows = _cumsum_lanes(lf_all[:, rw], tri)
        g_rows.append(li - b_rows)
        cm = g_rows[-1]
        sh = 1
        while sh < L:
            cm = jnp.maximum(cm, jnp.where(lane >= sh, pltpu.roll(cm, sh, 1), NEG_BIG))
            sh *= 2
        mx = jnp.maximum(m_prev, cm)
        b_end = b_rows[:, L - 1:L]
        log_ws = b_end - b_rows + li
        m_new = jnp.maximum(b_end + m_prev, jnp.max(log_ws, axis=-1, keepdims=True))
        decay.append(jnp.exp(b_end + m_prev - m_new))
        col_rows += [-mx, jnp.exp(m_prev - mx), jnp.exp(-(b_rows + mx)), jnp.exp(log_ws - m_new)]
        m_prev = m_new
    m_ref[...] = jnp.broadcast_to(m_prev, (H, LANES))
    cols = jnp.concatenate(col_rows + [jnp.zeros((LANES - 4 * H * len(rws), L), F32)], axis=0).T

    ones_col = (lax.broadcasted_iota(jnp.int32, (TB, LANES), 1) == 0).astype(BF16)
    vaug = [jnp.concatenate([vb[:, sl], ones_col], axis=1) for sl in sls]

    q = [_dot(xcb[:, sl], wq_ref[h]) for h, sl in zip(heads, sls)]
    k = [_dot(xcb[:, sl], wk_ref[h]) * (dh ** -0.5) for h, sl in zip(heads, sls)]
    qb = [t.astype(BF16) for t in q]
    kb = [t.astype(BF16) for t in k]
    units = [(j, h) for j in chunks for h in heads]
    col = lambda u, i: cols[:, (4 * u[0] + i) * H + u[1]:(4 * u[0] + i) * H + u[1] + 1]
    qk = {u: _dot_t(qb[u[1]][rws[u[0]]], kb[u[1]][rws[u[0]]]) for u in units}
    s_mat = {u: qk[u] * jnp.where(causal, jnp.exp(col(u, 0) + g_rows[u[0]][u[1]:u[1] + 1]), 0.0) for u in units}
    sv = {u: _dot(s_mat[u].astype(BF16), vaug[u[1]][rws[u[0]]]) for u in units}
    kw = {u: (k[u[1]][rws[u[0]]] * col(u, 3)).astype(BF16) for u in units}

    state = [c_ref[h] for h in heads]
    hs = {}
    for j in chunks:
        qc = [_dot(qb[h][rws[j]], state[h].astype(BF16)) for h in heads]
        for h in heads:
            u = (j, h)
            both = col(u, 1) * qc[h] + sv[u]
            inv = 1.0 / jnp.maximum(jnp.abs(both[:, dh:dh + 1]), col(u, 2))
            hs[u] = both[:, :dh] * inv
        state = [decay[j][h:h + 1] * state[h] + _tdot(kw[(j, h)], vaug[h][rws[j]]) for h in heads]
    for h in heads:
        c_ref[h] = state[h]

    gate = _gate_sigmoid(op_ref[0])
    nw, sk = nw_ref[...], sk_ref[...]
    for h, sl in zip(heads, sls):
        hg = jnp.concatenate([hs[(j, h)] for j in chunks], axis=0) * gate[:, sl]
        hg = hg - jnp.mean(hg, axis=-1, keepdims=True)
        hn = hg * lax.rsqrt(jnp.mean(hg * hg, axis=-1, keepdims=True) + EPS)
        y_ref[0, :, sl] = hn * nw[:, sl] + sk[:, sl] * xc[:, sl]


def _mlstm(z3, gates, bias, conv_w, conv_b, wq, wk, norm_w, skip, *, width):
    B, T, _ = z3.shape
    dh = width // ML_HEADS
    L = min(ML_BLOCK, T)
    vec = lambda b, t: (0, 0)
    return pl.pallas_call(
        _mlstm_kernel,
        out_shape=jax.ShapeDtypeStruct((B, T, width), F32),
        grid=(B, T // L),
        in_specs=[
            pl.BlockSpec(memory_space=pltpu.SMEM),
            pl.BlockSpec((1, L, width), lambda b, t: (b, t, 0)),
            pl.BlockSpec((1, L, width), lambda b, t: (b, t, 1)),
            pl.BlockSpec((1, 2 * ML_HEADS, L), lambda b, t: (b, 0, t)),
            pl.BlockSpec((ML_CONV, width), vec),
            pl.BlockSpec((1, width), vec),
            pl.BlockSpec((ML_HEADS, dh, dh), lambda b, t: (0, 0, 0)),
            pl.BlockSpec((ML_HEADS, dh, dh), lambda b, t: (0, 0, 0)),
            pl.BlockSpec((1, width), vec),
            pl.BlockSpec((1, width), vec),
        ],
        out_specs=pl.BlockSpec((1, L, width), lambda b, t: (b, t, 0)),
        scratch_shapes=[
            pltpu.VMEM((ML_HEADS, dh, dh + LANES), F32),
            pltpu.VMEM((ML_HEADS, LANES), F32),
            pltpu.VMEM((8, width), F32),
        ],
        compiler_params=_cparams(("arbitrary", "arbitrary")),
        name="mlstm",
    )(bias, z3, z3, gates, conv_w, conv_b.reshape(1, width), wq.astype(BF16),
      wk.astype(BF16), norm_w.reshape(1, width), skip.reshape(1, width))


def _midrow_broadcast(b, m):
    n, w = b.shape
    row = lambda i: b[i:i + 1]
    if m >= 8:
        return jnp.concatenate(
            [jnp.broadcast_to(row(p * 2 * m + m), (2 * m, w)) for p in range(n // (2 * m))], axis=0)
    pick = lambda off: jnp.concatenate(
        [jnp.broadcast_to(row(8 * g + off), (8, w)) for g in range(n // 8)], axis=0)
    sub = lax.broadcasted_iota(jnp.int32, (n, w), 0) % 8
    if m == 4:
        return pick(4)
    if m == 2:
        return jnp.where(sub < 4, pick(2), pick(6))
    return jnp.where(sub < 2, pick(1), jnp.where(sub < 4, pick(3), jnp.where(sub < 6, pick(5), pick(7))))


def _hgrn_kernel(q_ref, f_ref, i_ref, g_ref, lb_ref, nw_ref, y_ref, s_ref):
    C, W = q_ref.shape[1], q_ref.shape[2]
    H = W // HG_EXPAND

    @pl.when(pl.program_id(1) == 0)
    def _():
        s_ref[...] = jnp.zeros_like(s_ref)

    lb = lb_ref[...]
    sig = _sigmoid(f_ref[0])
    q = _silu(q_ref[0])
    k = (1.0 - lb) * (1.0 - sig)
    log_f = jnp.log(jnp.maximum(lb + (1.0 - lb) * sig, TINY))
    r = lax.broadcasted_iota(jnp.int32, (C, C), 0)
    c = lax.broadcasted_iota(jnp.int32, (C, C), 1)
    b = _cumsum_rows((r >= c).astype(BF16), log_f)
    v = i_ref[0]
    vb = v.astype(BF16)

    rowi = lax.broadcasted_iota(jnp.int32, (C, W), 0)
    qb, kb = q.astype(BF16), k.astype(BF16)
    zs, masks = [], []
    m = 1
    while m < C:
        e = jnp.exp(-jnp.abs(b - _midrow_broadcast(b, m)))
        zs.append(jnp.where((rowi % (2 * m)) >= m, qb, kb) * e.astype(BF16))
        masks.append(((r // (2 * m)) == (c // (2 * m))) & ((r % (2 * m)) >= m) & ((c % (2 * m)) < m))
        m *= 2

    qe = (q * jnp.exp(b)).astype(BF16)
    b_end = b[C - 1:C]
    kd = (k * jnp.exp(b_end - b)).astype(BF16)
    g_end = jnp.exp(b_end)
    diag = q * k
    gate = _gate_sigmoid(g_ref[0])
    nw = nw_ref[...]
    heads = range(H)
    sls = [slice(h * HG_EXPAND, (h + 1) * HG_EXPAND) for h in heads]
    st = [s_ref[h] for h in heads]
    inter = [_dot_t(qe[:, sl], st[h].astype(BF16)) for h, sl in zip(heads, sls)]
    for h, sl in zip(heads, sls):
        s_ref[h] = st[h] * g_end[:, sl] + _tdot(vb[:, sl], kd[:, sl])
    a = [jnp.zeros((C, C), F32)] * H
    for z_l, mask in zip(zs, masks):
        for h, sl in zip(heads, sls):
            a[h] = jnp.where(mask, _dot_t(z_l[:, sl], z_l[:, sl]), a[h])
    for h, sl in zip(heads, sls):
        o = (inter[h] + _dot(a[h].astype(BF16), vb[:, sl])
             + jnp.sum(diag[:, sl], axis=-1, keepdims=True) * v[:, sl])
        on = o * lax.rsqrt(jnp.mean(o * o, axis=-1, keepdims=True) + EPS)
        y_ref[0, :, sl] = on * nw[:, sl] * gate[:, sl]


def _hgrn(z3, lower_bound, norm_w, *, col0, width):
    B, T, _ = z3.shape
    C = min(HG_CHUNK, T)
    H = width // HG_EXPAND
    blk0 = col0 // width
    spec = lambda j: pl.BlockSpec((1, C, width), lambda b, t: (b, t, blk0 + j))
    vec = pl.BlockSpec((1, width), lambda b, t: (0, 0))
    return pl.pallas_call(
        _hgrn_kernel,
        out_shape=jax.ShapeDtypeStruct((B, T, width), F32),
        grid=(B, T // C),
        in_specs=[spec(0), spec(1), spec(2), spec(3), vec, vec],
        out_specs=pl.BlockSpec((1, C, width), lambda b, t: (b, t, 0)),
        scratch_shapes=[pltpu.VMEM((H, HG_EXPAND, HG_EXPAND), F32)],
        compiler_params=_cparams(("arbitrary", "arbitrary")),
        name="hgrn2",
    )(z3, z3, z3, z3, lower_bound.reshape(1, width), norm_w.reshape(1, width))


def _token_shift(z, mu, last_ref):
    n = z.shape[0]
    row = lax.broadcasted_iota(jnp.int32, z.shape, 0)
    prev = jnp.where(row == 0, last_ref[7:8], pltpu.roll(z, 1, 0))
    last_ref[...] = z[n - 8:n]
    return z + mu * (prev - z)


def _head_sum(x, ones_bd, two_pass=False):
    n, w = x.shape
    slab = ones_bd.shape[0]
    xs = jnp.concatenate([x[:, g * slab:(g + 1) * slab] for g in range(w // slab)], axis=0)
    hi = xs.astype(BF16)
    s = _dot(hi, ones_bd)
    if two_pass:
        s = s + _dot((xs - hi.astype(F32)).astype(BF16), ones_bd)
    return jnp.concatenate([s[g * n:(g + 1) * n] for g in range(w // slab)], axis=1)


def _softplus(x):
    return jnp.maximum(x, 0.0) + jnp.log1p(jnp.exp(-jnp.abs(x)))


def _rwkv_kernel(*refs, has_vres):
    it = iter(refs)
    r_ref, k_ref, v_ref, lo_ref = next(it), next(it), next(it), next(it)
    sm_ref, vf_ref = (next(it), next(it)) if has_vres else (None, None)
    mu_ref, mulo_ref, vec_ref, w2_ref, a2_ref, g2_ref = (next(it) for _ in range(6))
    musm_ref, v2_ref = (next(it), next(it)) if has_vres else (None, None)
    y_ref = next(it)
    vfo_ref = None if has_vres else next(it)
    s_ref, pr_ref, pk_ref, pv_ref, plo_ref = (next(it) for _ in range(5))
    psm_ref = next(it) if has_vres else None

    TB, W = r_ref.shape[1], r_ref.shape[2]
    C = min(RW_CHUNK, TB)
    P = W // LANES

    @pl.when(pl.program_id(1) == 0)
    def _():
        s_ref[...] = jnp.zeros_like(s_ref)
        for ref in (pr_ref, pk_ref, pv_ref, plo_ref, psm_ref):
            if ref is not None:
                ref[...] = jnp.zeros_like(ref)

    vec = vec_ref[...]
    w0, a0, k_k, k_a, ln_w, ln_b, r_k, v0 = (vec[i:i + 1] for i in range(8))
    mu = mu_ref[...]
    r = _token_shift(r_ref[0], mu[0:1], pr_ref)
    k = _token_shift(k_ref[0], mu[1:2], pk_ref)
    v = _token_shift(v_ref[0], mu[2:3], pv_ref)
    lo = _token_shift(lo_ref[0], mulo_ref[...], plo_ref)
    lo_wa, lo_g = lo[:, :LANES], lo[:, LANES:]

    w_raw = -_softplus(-(w0 + _dot(jnp.tanh(lo_wa).astype(BF16), w2_ref[...]))) - 0.5
    lw = -jnp.exp(w_raw)
    a = _gate_sigmoid(a0 + _dot(lo_wa.astype(BF16), a2_ref[...]))
    g = _dot(_gate_sigmoid(lo_g).astype(BF16), g2_ref[...])
    if has_vres:
        sm = _token_shift(sm_ref[0], musm_ref[...], psm_ref)
        v = v + (vf_ref[0] - v) * _gate_sigmoid(v0 + _dot(sm.astype(BF16), v2_ref[...]))
    else:
        vfo_ref[0] = v

    rr2 = lax.broadcasted_iota(jnp.int32, (2 * LANES, 2 * LANES), 0)
    cc2 = lax.broadcasted_iota(jnp.int32, (2 * LANES, 2 * LANES), 1)
    ones_bd = ((rr2 // RW_HEAD) == (cc2 // RW_HEAD)).astype(BF16)
    rr = lax.broadcasted_iota(jnp.int32, (LANES, LANES), 0)
    cc = lax.broadcasted_iota(jnp.int32, (LANES, LANES), 1)

    kk = k * k_k
    kmod = k * (1.0 + (a - 1.0) * k_a)
    stats = _head_sum(jnp.concatenate([kk * kk, r * kmod * r_k], axis=0), ones_bd)
    kk = kk / jnp.maximum(jnp.sqrt(stats[:TB]), 1e-12)
    bonus_dot = stats[TB:]
    ahat = kk * a

    tr = lax.broadcasted_iota(jnp.int32, (TB, TB), 0)
    tc = lax.broadcasted_iota(jnp.int32, (TB, TB), 1)
    cum = _cumsum_rows(((tr >= tc) & ((tr // C) == (tc // C))).astype(BF16), lw)
    e_inc = jnp.exp(cum)
    e_inv = jnp.exp(-cum)
    r_dec = r * e_inc
    kk_dec = kk * jnp.exp(cum - lw)
    a_inv = ahat * e_inv
    k_inv = kmod * e_inv

    lane = lax.broadcasted_iota(jnp.int32, (C, LANES), 1)
    first = lane < RW_HEAD

    def stack(zp):
        return jnp.concatenate([jnp.where(first, zp, 0.0), jnp.where(first, 0.0, zp)], axis=0).astype(BF16)

    strict = rr > cc
    incl = rr >= cc
    eye = (rr == cc).astype(F32)
    n_sq = int(np.log2(C)) - 1
    n2 = 2 * C
    units = [(j, p) for j in range(TB // C) for p in range(P)]
    cut = lambda t, u: t[u[0] * C:(u[0] + 1) * C, u[1] * LANES:(u[1] + 1) * LANES]
    kks = {u: stack(cut(kk_dec, u)) for u in units}
    rs = {u: stack(cut(r_dec, u)) for u in units}
    vs = {u: stack(cut(v, u)) for u in units}
    ak = {u: jnp.concatenate([stack(cut(a_inv, u)), stack(cut(k_inv, u))], axis=0) for u in units}
    sc = {u: _dot_t(jnp.concatenate([kks[u], rs[u]], axis=0), ak[u]) for u in units}
    l_a = {u: jnp.where(strict, sc[u][:n2, :n2], 0.0) for u in units}
    lkv = {u: _dot(jnp.where(strict, sc[u][:n2, n2:], 0.0).astype(BF16), vs[u]) for u in units}
    a_cat = {u: jnp.where(jnp.concatenate([incl, incl], axis=1), sc[u][n2:], 0.0).astype(BF16) for u in units}
    t_inv = {u: eye - l_a[u] for u in units}
    pw = {u: _dot(l_a[u].astype(BF16), l_a[u].astype(BF16)).astype(BF16) for u in units}
    for _ in range(n_sq - 1):
        both = {u: _dot(pw[u], jnp.concatenate([t_inv[u].astype(BF16), pw[u]], axis=1)) for u in units}
        t_inv = {u: t_inv[u] + both[u][:, :n2] for u in units}
        pw = {u: both[u][:, n2:].astype(BF16) for u in units}
    t_inv = {u: t_inv[u] + _dot(pw[u], t_inv[u].astype(BF16)) for u in units}
    rhs = {u: jnp.concatenate([-kks[u].astype(F32), -lkv[u]], axis=1).astype(BF16) for u in units}
    wv = {u: _dot(t_inv[u].astype(BF16), rhs[u]) for u in units}

    st = [s_ref[p] for p in range(P)]
    y_rows = []
    for j in range(TB // C):
        us = [(j, p) for p in range(P)]
        wr = [_dot_t(jnp.concatenate([wv[u][:, :LANES].astype(BF16), rs[u]], axis=0), st[u[1]].astype(BF16))
              for u in us]
        uv = [jnp.concatenate([(wr[p][:n2] + wv[u][:, LANES:]).astype(BF16), vs[u]], axis=0)
              for p, u in enumerate(us)]
        yst = [wr[p][n2:] + _dot(a_cat[u], uv[p]) for p, u in enumerate(us)]
        gamma_end = e_inc[(j + 1) * C - 1:(j + 1) * C]
        st = [(st[p] + _tdot(uv[p], ak[u])) * gamma_end[:, p * LANES:(p + 1) * LANES]
              for p, u in enumerate(us)]
        y_rows.append(jnp.concatenate([yst[p][:C] + yst[p][C:] for p in range(P)], axis=1))
    for p in range(P):
        s_ref[p] = st[p]
    y = jnp.concatenate(y_rows, axis=0)

    inv_n = 1.0 / RW_HEAD
    yc = y - _head_sum(y, ones_bd, two_pass=True) * inv_n
    yn = yc * lax.rsqrt(_head_sum(yc * yc, ones_bd) * inv_n + RW_LN_EPS) * ln_w + ln_b
    y_ref[0] = (yn + bonus_dot * v) * g


def _rwkv(z3, v_first, prm, *, rkv_col0, lora_col0, small_col0, width):
    B, T, _ = z3.shape
    C = min(RW_BLOCK, T)
    has_vres = v_first is not None
    lora_w = RW_DECAY_LORA + RW_AAA_LORA + RW_GATE_LORA
    tok = lambda wdt, col0, j=0: pl.BlockSpec((1, C, wdt), lambda b, t: (b, t, col0 // wdt + j))
    full = lambda arr: pl.BlockSpec(arr.shape, lambda b, t: (0,) * arr.ndim)
    tok_out = pl.BlockSpec((1, C, width), lambda b, t: (b, t, 0))

    args = [z3, z3, z3, z3]
    specs = [tok(width, rkv_col0, 0), tok(width, rkv_col0, 1), tok(width, rkv_col0, 2), tok(lora_w, lora_col0)]
    if has_vres:
        args += [z3, v_first]
        specs += [tok(LANES, small_col0), tok_out]
    weights = [prm["mu_rkv"], prm["mu_lora"], prm["vecs"], prm["w2"], prm["a2"], prm["g2"]]
    if has_vres:
        weights += [prm["mu_small"], prm["v2"]]
    args += weights
    specs += [full(wt) for wt in weights]

    y_shape = jax.ShapeDtypeStruct((B, T, width), F32)
    scratch = [pltpu.VMEM((width // LANES, LANES, LANES), F32)]
    scratch += [pltpu.VMEM((8, width), F32)] * 3 + [pltpu.VMEM((8, lora_w), F32)]
    if has_vres:
        scratch += [pltpu.VMEM((8, LANES), F32)]
    out = pl.pallas_call(
        functools.partial(_rwkv_kernel, has_vres=has_vres),
        out_shape=y_shape if has_vres else (y_shape, y_shape),
        grid=(B, T // C),
        in_specs=specs,
        out_specs=tok_out if has_vres else (tok_out, tok_out),
        scratch_shapes=scratch,
        compiler_params=_cparams(("arbitrary", "arbitrary")),
        name="rwkv7",
    )(*args)
    return (out, v_first) if has_vres else out


def _pad_rows(w, row0, rows):
    return jnp.zeros((rows, w.shape[1]), w.dtype).at[row0:row0 + w.shape[0]].set(w)


def kernel(x, c, norm_w, final_norm_w, ada_w, ada_b, ffn_up, ffn_down, w_in, w_in_vres,
           ml_conv_w, ml_conv_b, ml_wq, ml_wk, ml_i_b, ml_f_b, ml_norm_w, ml_skip,
           rw_mu, rw_mu_vres, rw_w0, rw_w2, rw_a0, rw_a2, rw_v0, rw_v2, rw_g2,
           rw_k_k, rw_k_a, rw_r_k, rw_ln_w, rw_ln_b, hg_lb_logits, hg_norm_w,
           w_branch, w_out):
    B, T, D = x.shape
    depth = norm_w.shape[0]
    N = B * T
    W = D
    lora_w = RW_DECAY_LORA + RW_AAA_LORA + RW_GATE_LORA

    o_mlx, o_mlo, o_mli, o_mlf = 0, W, 2 * W, 2 * W + ML_HEADS
    o_rw = 2 * W + 2 * ML_HEADS
    o_hg = o_rw + 3 * W + lora_w
    o_gate = o_hg + 4 * W
    n_in = o_gate + N_BRANCH * D
    c_rkv, c_hg = 2 * W, 5 * W
    c_lora = c_hg + 4 * W
    c_small = c_lora + lora_w
    n_cols = c_small + LANES
    s_vlo = 2 * ML_HEADS

    lb_soft = jax.nn.softmax(hg_lb_logits.astype(F32), axis=0)
    lower_bounds = jnp.cumsum(lb_soft, axis=0) - lb_soft[0]

    mod_all = _ada_mod(c, ada_w, ada_b).reshape(depth, B, N_SUB * 3, D)
    nw3 = norm_w.reshape(depth * N_SUB, 1, D)
    ffn_up_b, ffn_down_b = ffn_up.astype(BF16), ffn_down.astype(BF16)
    w_branch_b, w_out_b = w_branch.astype(BF16), w_out.astype(BF16)
    x2 = x.reshape(N, D)
    v_first = None
    for l in range(depth):
        wl = w_in[l]
        small = jnp.concatenate([wl[:, o_mli:o_mli + 2 * ML_HEADS]]
                                + ([w_in_vres[l - 1]] if l > 0 else []), axis=1)
        small = jnp.pad(small, ((0, 0), (0, LANES - small.shape[1])))
        w_cat = jnp.concatenate([
            wl[:, o_mlx:o_mlx + 2 * W], wl[:, o_rw:o_rw + 3 * W], wl[:, o_hg:o_hg + 4 * W],
            wl[:, o_rw + 3 * W:o_rw + 3 * W + lora_w], small], axis=1).astype(BF16)

        x2 = _ffn(x2, mod_all, nw3, ffn_up_b, ffn_down_b, final_norm_w,
                  layer=l, half=0, tokens_per_batch=T, final_norm=False)

        z = _inproj(x2, mod_all, nw3, w_cat, layer=l, tokens_per_batch=T)
        z3 = z.reshape(B, T, n_cols)

        gates = z3[:, :, c_small:c_small + 2 * ML_HEADS].transpose(0, 2, 1)
        y_ml = _mlstm(z3, gates, jnp.concatenate([ml_i_b[l], ml_f_b[l]]), ml_conv_w[l], ml_conv_b[l],
                      ml_wq[l], ml_wk[l], ml_norm_w[l], ml_skip[l], width=W)

        mu = rw_mu[l]
        prm = {
            "mu_rkv": mu[:3 * W].reshape(3, W),
            "mu_lora": mu[3 * W:].reshape(1, lora_w),
            "vecs": jnp.stack([rw_w0[l], rw_a0[l], rw_k_k[l], rw_k_a[l], rw_ln_w[l], rw_ln_b[l],
                               rw_r_k[l].reshape(W), rw_v0[l - 1] if l > 0 else jnp.zeros((W,), F32)]),
            "w2": _pad_rows(rw_w2[l], 0, LANES).astype(BF16),
            "a2": _pad_rows(rw_a2[l], RW_DECAY_LORA, LANES).astype(BF16),
            "g2": rw_g2[l].astype(BF16),
        }
        if l > 0:
            prm["mu_small"] = jnp.zeros((1, LANES), F32).at[0, s_vlo:s_vlo + RW_MV_LORA].set(rw_mu_vres[l - 1])
            prm["v2"] = _pad_rows(rw_v2[l - 1], s_vlo, LANES).astype(BF16)
        y_rw, v_first = _rwkv(z3, v_first, prm, rkv_col0=c_rkv, lora_col0=c_lora,
                              small_col0=c_small, width=W)

        y_hg = _hgrn(z3, lower_bounds[l], hg_norm_w[l], col0=c_hg, width=W)

        x2 = _branch_mix(x2, mod_all, nw3, y_ml.reshape(N, W), y_rw.reshape(N, W), y_hg.reshape(N, W),
                         wl[:, o_gate:n_in].astype(BF16), w_branch_b, w_out_b, layer=l, tokens_per_batch=T)

        x2 = _ffn(x2, mod_all, nw3, ffn_up_b, ffn_down_b, final_norm_w,
                  layer=l, half=1, tokens_per_batch=T, final_norm=(l == depth - 1))
    return x2.reshape(B, T, D)
```

```python
import functools

import numpy as np
import jax
import jax.numpy as jnp
from jax import lax
from jax.experimental import pallas as pl
from jax.experimental.pallas import tpu as pltpu

F32 = jnp.float32
BF16 = jnp.bfloat16

EPS = 1e-6
NEG_BIG = -1e30
TINY = 1e-30

N_SUB = 3
N_BRANCH = 3
ML_HEADS = 4
ML_CONV = 4
RW_HEAD = 64
RW_LN_EPS = 64e-5
RW_DECAY_LORA = 64
RW_AAA_LORA = 64
RW_MV_LORA = 32
RW_GATE_LORA = 128
HG_EXPAND = 128

LANES = 128
VMEM_LIMIT = 56 * 1024 * 1024

FFN_TM = 512
FFN_FC = 256
INP_TM = 1024
INP_TN = 1920
MIX_TM = 512
ML_CHUNK = 256
ML_BLOCK = 512
RW_CHUNK = 64
RW_BLOCK = 256
HG_CHUNK = 128


def _cparams(sem):
    return pltpu.CompilerParams(dimension_semantics=sem, vmem_limit_bytes=VMEM_LIMIT)


def _dot(a, b):
    return jnp.dot(a, b, preferred_element_type=F32)


def _dot_t(a, b):
    return lax.dot_general(a, b, (((1,), (1,)), ((), ())), preferred_element_type=F32)


def _tdot(a, b):
    return lax.dot_general(a, b, (((0,), (0,)), ((), ())), preferred_element_type=F32)


def _split3(x):
    hi = x.astype(BF16)
    r1 = x - hi.astype(F32)
    mid = r1.astype(BF16)
    lo = (r1 - mid.astype(F32)).astype(BF16)
    return hi, mid, lo


def _cumsum_rows(tri, x):
    hi, mid, lo = _split3(x)
    return _dot(tri, hi) + (_dot(tri, mid) + _dot(tri, lo))


def _cumsum_lanes(x, tri):
    hi, mid, lo = _split3(x)
    return _dot(hi, tri) + (_dot(mid, tri) + _dot(lo, tri))


def _sigmoid(x):
    return 1.0 / (1.0 + jnp.exp(-x))


def _gate_sigmoid(x):
    return 0.5 * jnp.tanh(0.5 * x) + 0.5


def _silu(x):
    return x * _gate_sigmoid(x)


def _modulated_norm(x, nw, shift, scale):
    ms = jnp.mean(x * x, axis=-1, keepdims=True)
    return (x * lax.rsqrt(ms + EPS) * nw) * (1.0 + scale) + shift


def _ada_kernel(c_ref, w_ref, b_ref, o_ref):
    cond = _silu(c_ref[...])
    o_ref[0] = _dot(cond, w_ref[0]) + b_ref[0]


def _ada_mod(c, ada_w, ada_b):
    L, D, M = ada_w.shape
    B = c.shape[0]
    tn = M // 8
    return pl.pallas_call(
        _ada_kernel,
        out_shape=jax.ShapeDtypeStruct((L, B, M), F32),
        grid=(L, M // tn),
        in_specs=[
            pl.BlockSpec((B, D), lambda l, j: (0, 0)),
            pl.BlockSpec((1, D, tn), lambda l, j: (l, 0, j)),
            pl.BlockSpec((1, 1, tn), lambda l, j: (l, 0, j)),
        ],
        out_specs=pl.BlockSpec((1, B, tn), lambda l, j: (l, 0, j)),
        compiler_params=_cparams(("arbitrary", "arbitrary")),
        name="ada_mod",
    )(c, ada_w, ada_b.reshape(L, 1, M))


def _ffn_kernel(x_ref, mod_ref, nw_ref, wu_ref, wd_ref, fw_ref, o_ref, acc_ref, *, sub, final_norm):
    x = x_ref[...]
    mod = mod_ref[0, 0]
    shift, scale, gate = (mod[3 * sub + i:3 * sub + i + 1] for i in range(3))
    hb = _modulated_norm(x, nw_ref[0], shift, scale).astype(BF16)
    F = wd_ref.shape[2]
    for c in range(F // FFN_FC):
        cols = slice(c * FFN_FC, (c + 1) * FFN_FC)
        a = _dot(hb, wu_ref[0, 0, :, cols])
        b = _dot(hb, wu_ref[0, 0, :, F + c * FFN_FC:F + (c + 1) * FFN_FC])
        act = (_silu(a) * b).astype(BF16)
        d = _dot(act, wd_ref[0, 0, cols, :])
        if c == 0:
            acc_ref[...] = d
        else:
            acc_ref[...] += d
    y = x + (0.5 * (1.0 + gate)) * acc_ref[...]
    if final_norm:
        ms = jnp.mean(y * y, axis=-1, keepdims=True)
        y = y * lax.rsqrt(ms + EPS) * fw_ref[...]
    o_ref[...] = y


def _ffn(x, mod_all, norm_w, w_up, w_down, fw, *, layer, half, tokens_per_batch, final_norm):
    N, D = x.shape
    sub = 2 * half
    bpb = tokens_per_batch // FFN_TM
    return pl.pallas_call(
        functools.partial(_ffn_kernel, sub=sub, final_norm=final_norm),
        out_shape=jax.ShapeDtypeStruct((N, D), F32),
        grid=(N // FFN_TM,),
        in_specs=[
            pl.BlockSpec((FFN_TM, D), lambda i: (i, 0)),
            pl.BlockSpec((1, 1) + mod_all.shape[2:], lambda i: (layer, i // bpb, 0, 0)),
            pl.BlockSpec((1, 1, D), lambda i: (N_SUB * layer + sub, 0, 0)),
            pl.BlockSpec((1, 1) + w_up.shape[2:], lambda i: (layer, half, 0, 0), pipeline_mode=pl.Buffered(1)),
            pl.BlockSpec((1, 1) + w_down.shape[2:], lambda i: (layer, half, 0, 0), pipeline_mode=pl.Buffered(1)),
            pl.BlockSpec((1, D), lambda i: (0, 0)),
        ],
        out_specs=pl.BlockSpec((FFN_TM, D), lambda i: (i, 0)),
        scratch_shapes=[pltpu.VMEM((FFN_TM, D), F32)],
        compiler_params=_cparams(("arbitrary",)),
        name="ffn_half",
    )(x, mod_all, norm_w, w_up, w_down, fw.reshape(1, D))


def _inproj_kernel(x_ref, mod_ref, nw_ref, w_ref, o_ref, h_ref, *, sub):
    @pl.when(pl.program_id(1) == 0)
    def _():
        mod = mod_ref[0, 0]
        shift, scale = mod[3 * sub:3 * sub + 1], mod[3 * sub + 1:3 * sub + 2]
        h_ref[...] = _modulated_norm(x_ref[...], nw_ref[0], shift, scale).astype(BF16)

    o_ref[...] = _dot(h_ref[...], w_ref[pl.program_id(1)])


def _inproj(x, mod_all, norm_w, w, *, layer, tokens_per_batch):
    N, D = x.shape
    n_tiles = w.shape[0]
    sub = 1
    tm = min(INP_TM, tokens_per_batch)
    bpb = tokens_per_batch // tm
    return pl.pallas_call(
        functools.partial(_inproj_kernel, sub=sub),
        out_shape=jax.ShapeDtypeStruct((N, n_tiles * INP_TN), F32),
        grid=(N // tm, n_tiles),
        in_specs=[
            pl.BlockSpec((tm, D), lambda i, j: (i, 0)),
            pl.BlockSpec((1, 1) + mod_all.shape[2:], lambda i, j: (layer, i // bpb, 0, 0)),
            pl.BlockSpec((1, 1, D), lambda i, j: (N_SUB * layer + sub, 0, 0)),
            pl.BlockSpec(w.shape, lambda i, j: (0, 0, 0), pipeline_mode=pl.Buffered(1)),
        ],
        out_specs=pl.BlockSpec((tm, INP_TN), lambda i, j: (i, j)),
        scratch_shapes=[pltpu.VMEM((tm, D), BF16)],
        compiler_params=_cparams(("arbitrary", "arbitrary")),
        name="mixer_inproj",
    )(x, mod_all, norm_w, w)


def _mix_kernel(x_ref, mod_ref, nw_ref, yml_ref, yrw_ref, yhg_ref, wg_ref, wbr_ref, wout_ref, o_ref,
                *, sub):
    x = x_ref[...]
    D = x.shape[1]
    mod = mod_ref[0, 0]
    shift, scale, gate = (mod[3 * sub + i:3 * sub + i + 1] for i in range(3))
    hb = _modulated_norm(x, nw_ref[0], shift, scale).astype(BF16)
    mixed = None
    for n, y_ref in enumerate((yml_ref, yrw_ref, yhg_ref)):
        gate_pre = _dot(hb, wg_ref[:, n * D:(n + 1) * D])
        term = _gate_sigmoid(gate_pre) * _dot(y_ref[...].astype(BF16), wbr_ref[0, n])
        mixed = term if mixed is None else mixed + term
    o_ref[...] = x + (1.0 + gate) * _dot(mixed.astype(BF16), wout_ref[0])


def _branch_mix(x, mod_all, norm_w, y_ml, y_rw, y_hg, w_gate, w_branch, w_out, *, layer, tokens_per_batch):
    N, D = x.shape
    sub = 1
    bpb = tokens_per_batch // MIX_TM
    tok = pl.BlockSpec((MIX_TM, D), lambda i: (i, 0))
    return pl.pallas_call(
        functools.partial(_mix_kernel, sub=sub),
        out_shape=jax.ShapeDtypeStruct((N, D), F32),
        grid=(N // MIX_TM,),
        in_specs=[
            tok,
            pl.BlockSpec((1, 1) + mod_all.shape[2:], lambda i: (layer, i // bpb, 0, 0)),
            pl.BlockSpec((1, 1, D), lambda i: (N_SUB * layer + sub, 0, 0)),
            tok, tok, tok,
            pl.BlockSpec(w_gate.shape, lambda i: (0, 0), pipeline_mode=pl.Buffered(1)),
            pl.BlockSpec((1,) + w_branch.shape[1:], lambda i: (layer, 0, 0, 0), pipeline_mode=pl.Buffered(1)),
            pl.BlockSpec((1,) + w_out.shape[1:], lambda i: (layer, 0, 0), pipeline_mode=pl.Buffered(1)),
        ],
        out_specs=tok,
        compiler_params=_cparams(("arbitrary",)),
        name="branch_mix",
    )(x, mod_all, norm_w, y_ml, y_rw, y_hg, w_gate, w_branch, w_out)


def _log_sigmoid(x):
    return jnp.minimum(x, 0.0) - jnp.log1p(jnp.exp(-jnp.abs(x)))


def _mlstm_kernel(bias_ref, x_ref, op_ref, g_ref, cw_ref, cb_ref, wq_ref, wk_ref,
                  nw_ref, sk_ref, y_ref, c_ref, m_ref, tail_ref):
    TB, W = x_ref.shape[1], x_ref.shape[2]
    L = min(ML_CHUNK, TB)
    H = ML_HEADS
    dh = W // H
    heads = range(H)
    chunks = range(TB // L)
    sls = [slice(h * dh, (h + 1) * dh) for h in heads]
    rws = [slice(j * L, (j + 1) * L) for j in chunks]

    @pl.when(pl.program_id(1) == 0)
    def _():
        c_ref[...] = jnp.zeros_like(c_ref)
        m_ref[...] = jnp.zeros_like(m_ref)
        tail_ref[...] = jnp.zeros_like(tail_ref)

    x = x_ref[0]
    cw = cw_ref[...]
    tail = tail_ref[...]
    row8 = lax.broadcasted_iota(jnp.int32, (8, W), 0)
    conv = x * cw[ML_CONV - 1:ML_CONV]
    for s in range(1, ML_CONV):
        xr = pltpu.roll(x, s, 0)
        top = jnp.where(row8 < s, pltpu.roll(tail, s, 0), xr[0:8])
        xs = jnp.concatenate([top, xr[8:]], axis=0)
        conv = conv + xs * cw[ML_CONV - 1 - s:ML_CONV - s]
    tail_ref[...] = x[TB - 8:TB]
    xc = _silu(conv + cb_ref[...])
    xcb = xc.astype(BF16)
    vb = x.astype(BF16)

    g = g_ref[0]
    li_all = jnp.concatenate([g[h:h + 1] + bias_ref[h] for h in heads], axis=0)
    lf_all = _log_sigmoid(jnp.concatenate([g[H + h:H + h + 1] + bias_ref[H + h] for h in heads], axis=0))
    r = lax.broadcasted_iota(jnp.int32, (L, L), 0)
    c = lax.broadcasted_iota(jnp.int32, (L, L), 1)
    causal = r >= c
    tri = (r <= c).astype(BF16)

    lane = lax.broadcasted_iota(jnp.int32, (H, L), 1)
    m_prev = m_ref[...][:, 0:1]
    g_rows, decay, col_rows = [], [], []
    for rw in rws:
        li = li_all[:, rw]
        b_rows = _cumsum_lanes(lf_all[:, rw], tri)
        g_rows.append(li - b_rows)
        cm = g_rows[-1]
        sh = 1
        while sh < L:
            cm = jnp.maximum(cm, jnp.where(lane >= sh, pltpu.roll(cm, sh, 1), NEG_BIG))
            sh *= 2
        mx = jnp.maximum(m_prev, cm)
        b_end = b_rows[:, L - 1:L]
        log_ws = b_end - b_rows + li
        m_new = jnp.maximum(b_end + m_prev, jnp.max(log_ws, axis=-1, keepdims=True))
        decay.append(jnp.exp(b_end + m_prev - m_new))
        col_rows += [-mx, jnp.exp(m_prev - mx), jnp.exp(-(b_rows + mx)), jnp.exp(log_ws - m_new)]
        m_prev = m_new
    m_ref[...] = jnp.broadcast_to(m_prev, (H, LANES))
    cols = jnp.concatenate(col_rows + [jnp.zeros((LANES - 4 * H * len(rws), L), F32)], axis=0).T

    ones_col = (lax.broadcasted_iota(jnp.int32, (TB, LANES), 1) == 0).astype(BF16)
    vaug = [jnp.concatenate([vb[:, sl], ones_col], axis=1) for sl in sls]

    q = [_dot(xcb[:, sl], wq_ref[h]) for h, sl in zip(heads, sls)]
    k = [_dot(xcb[:, sl], wk_ref[h]) * (dh ** -0.5) for h, sl in zip(heads, sls)]
    qb = [t.astype(BF16) for t in q]
    kb = [t.astype(BF16) for t in k]
    units = [(j, h) for j in chunks for h in heads]
    col = lambda u, i: cols[:, (4 * u[0] + i) * H + u[1]:(4 * u[0] + i) * H + u[1] + 1]
    qk = {u: _dot_t(qb[u[1]][rws[u[0]]], kb[u[1]][rws[u[0]]]) for u in units}
    s_mat = {u: qk[u] * jnp.where(causal, jnp.exp(col(u, 0) + g_rows[u[0]][u[1]:u[1] + 1]), 0.0) for u in units}
    sv = {u: _dot(s_mat[u].astype(BF16), vaug[u[1]][rws[u[0]]]) for u in units}
    kw = {u: (k[u[1]][rws[u[0]]] * col(u, 3)).astype(BF16) for u in units}

    state = [c_ref[h] for h in heads]
    hs = {}
    for j in chunks:
        qc = [_dot(qb[h][rws[j]], state[h].astype(BF16)) for h in heads]
        for h in heads:
            u = (j, h)
            both = col(u, 1) * qc[h] + sv[u]
            inv = 1.0 / jnp.maximum(jnp.abs(both[:, dh:dh + 1]), col(u, 2))
            hs[u] = both[:, :dh] * inv
        state = [decay[j][h:h + 1] * state[h] + _tdot(kw[(j, h)], vaug[h][rws[j]]) for h in heads]
    for h in heads:
        c_ref[h] = state[h]

    gate = _gate_sigmoid(op_ref[0])
    nw, sk = nw_ref[...], sk_ref[...]
    for h, sl in zip(heads, sls):
        hg = jnp.concatenate([hs[(j, h)] for j in chunks], axis=0) * gate[:, sl]
        hg = hg - jnp.mean(hg, axis=-1, keepdims=True)
        hn = hg * lax.rsqrt(jnp.mean(hg * hg, axis=-1, keepdims=True) + EPS)
        y_ref[0, :, sl] = hn * nw[:, sl] + sk[:, sl] * xc[:, sl]


def _mlstm(z3, gates, bias, conv_w, conv_b, wq, wk, norm_w, skip, *, width):
    B, T, _ = z3.shape
    dh = width // ML_HEADS
    L = min(ML_BLOCK, T)
    vec = lambda b, t: (0, 0)
    return pl.pallas_call(
        _mlstm_kernel,
        out_shape=jax.ShapeDtypeStruct((B, T, width), F32),
        grid=(B, T // L),
        in_specs=[
            pl.BlockSpec(memory_space=pltpu.SMEM),
            pl.BlockSpec((1, L, width), lambda b, t: (b, t, 0)),
            pl.BlockSpec((1, L, width), lambda b, t: (b, t, 1)),
            pl.BlockSpec((1, 2 * ML_HEADS, L), lambda b, t: (b, 0, t)),
            pl.BlockSpec((ML_CONV, width), vec),
            pl.BlockSpec((1, width), vec),
            pl.BlockSpec((ML_HEADS, dh, dh), lambda b, t: (0, 0, 0)),
            pl.BlockSpec((ML_HEADS, dh, dh), lambda b, t: (0, 0, 0)),
            pl.BlockSpec((1, width), vec),
            pl.BlockSpec((1, width), vec),
        ],
        out_specs=pl.BlockSpec((1, L, width), lambda b, t: (b, t, 0)),
        scratch_shapes=[
            pltpu.VMEM((ML_HEADS, dh, dh + LANES), F32),
            pltpu.VMEM((ML_HEADS, LANES), F32),
            pltpu.VMEM((8, width), F32),
        ],
        compiler_params=_cparams(("arbitrary", "arbitrary")),
        name="mlstm",
    )(bias, z3, z3, gates, conv_w, conv_b.reshape(1, width), wq.astype(BF16),
      wk.astype(BF16), norm_w.reshape(1, width), skip.reshape(1, width))


def _midrow_broadcast(b, m):
    n, w = b.shape
    row = lambda i: b[i:i + 1]
    if m >= 8:
        return jnp.concatenate(
            [jnp.broadcast_to(row(p * 2 * m + m), (2 * m, w)) for p in range(n // (2 * m))], axis=0)
    pick = lambda off: jnp.concatenate(
        [jnp.broadcast_to(row(8 * g + off), (8, w)) for g in range(n // 8)], axis=0)
    sub = lax.broadcasted_iota(jnp.int32, (n, w), 0) % 8
    if m == 4:
        return pick(4)
    if m == 2:
        return jnp.where(sub < 4, pick(2), pick(6))
    return jnp.where(sub < 2, pick(1), jnp.where(sub < 4, pick(3), jnp.where(sub < 6, pick(5), pick(7))))


def _hgrn_kernel(q_ref, f_ref, i_ref, g_ref, lb_ref, nw_ref, y_ref, s_ref):
    C, W = q_ref.shape[1], q_ref.shape[2]
    H = W // HG_EXPAND

    @pl.when(pl.program_id(1) == 0)
    def _():
        s_ref[...] = jnp.zeros_like(s_ref)

    lb = lb_ref[...]
    sig = _sigmoid(f_ref[0])
    q = _silu(q_ref[0])
    k = (1.0 - lb) * (1.0 - sig)
    log_f = jnp.log(jnp.maximum(lb + (1.0 - lb) * sig, TINY))
    r = lax.broadcasted_iota(jnp.int32, (C, C), 0)
    c = lax.broadcasted_iota(jnp.int32, (C, C), 1)
    b = _cumsum_rows((r >= c).astype(BF16), log_f)
    v = i_ref[0]
    vb = v.astype(BF16)

    rowi = lax.broadcasted_iota(jnp.int32, (C, W), 0)
    qb, kb = q.astype(BF16), k.astype(BF16)
    zs, masks = [], []
    m = 1
    while m < C:
        e = jnp.exp(-jnp.abs(b - _midrow_broadcast(b, m)))
        zs.append(jnp.where((rowi % (2 * m)) >= m, qb, kb) * e.astype(BF16))
        masks.append(((r // (2 * m)) == (c // (2 * m))) & ((r % (2 * m)) >= m) & ((c % (2 * m)) < m))
        m *= 2

    qe = (q * jnp.exp(b)).astype(BF16)
    b_end = b[C - 1:C]
    kd = (k * jnp.exp(b_end - b)).astype(BF16)
    g_end = jnp.exp(b_end)
    diag = q * k
    gate = _gate_sigmoid(g_ref[0])
    nw = nw_ref[...]
    heads = range(H)
    sls = [slice(h * HG_EXPAND, (h + 1) * HG_EXPAND) for h in heads]
    st = [s_ref[h] for h in heads]
    inter = [_dot_t(qe[:, sl], st[h].astype(BF16)) for h, sl in zip(heads, sls)]
    for h, sl in zip(heads, sls):
        s_ref[h] = st[h] * g_end[:, sl] + _tdot(vb[:, sl], kd[:, sl])
    a = [jnp.zeros((C, C), F32)] * H
    for z_l, mask in zip(zs, masks):
        for h, sl in zip(heads, sls):
            a[h] = jnp.where(mask, _dot_t(z_l[:, sl], z_l[:, sl]), a[h])
    for h, sl in zip(heads, sls):
        o = (inter[h] + _dot(a[h].astype(BF16), vb[:, sl])
             + jnp.sum(diag[:, sl], axis=-1, keepdims=True) * v[:, sl])
        on = o * lax.rsqrt(jnp.mean(o * o, axis=-1, keepdims=True) + EPS)
        y_ref[0, :, sl] = on * nw[:, sl] * gate[:, sl]


def _hgrn(z3, lower_bound, norm_w, *, col0, width):
    B, T, _ = z3.shape
    C = min(HG_CHUNK, T)
    H = width // HG_EXPAND
    blk0 = col0 // width
    spec = lambda j: pl.BlockSpec((1, C, width), lambda b, t: (b, t, blk0 + j))
    vec = pl.BlockSpec((1, width), lambda b, t: (0, 0))
    return pl.pallas_call(
        _hgrn_kernel,
        out_shape=jax.ShapeDtypeStruct((B, T, width), F32),
        grid=(B, T // C),
        in_specs=[spec(0), spec(1), spec(2), spec(3), vec, vec],
        out_specs=pl.BlockSpec((1, C, width), lambda b, t: (b, t, 0)),
        scratch_shapes=[pltpu.VMEM((H, HG_EXPAND, HG_EXPAND), F32)],
        compiler_params=_cparams(("arbitrary", "arbitrary")),
        name="hgrn2",
    )(z3, z3, z3, z3, lower_bound.reshape(1, width), norm_w.reshape(1, width))


def _token_shift(z, mu, last_ref):
    n = z.shape[0]
    row = lax.broadcasted_iota(jnp.int32, z.shape, 0)
    prev = jnp.where(row == 0, last_ref[7:8], pltpu.roll(z, 1, 0))
    last_ref[...] = z[n - 8:n]
    return z + mu * (prev - z)


def _head_sum(x, ones_bd, two_pass=False):
    n, w = x.shape
    slab = ones_bd.shape[0]
    xs = jnp.concatenate([x[:, g * slab:(g + 1) * slab] for g in range(w // slab)], axis=0)
    hi = xs.astype(BF16)
    s = _dot(hi, ones_bd)
    if two_pass:
        s = s + _dot((xs - hi.astype(F32)).astype(BF16), ones_bd)
    return jnp.concatenate([s[g * n:(g + 1) * n] for g in range(w // slab)], axis=1)


def _softplus(x):
    return jnp.maximum(x, 0.0) + jnp.log1p(jnp.exp(-jnp.abs(x)))


def _rwkv_kernel(*refs, has_vres):
    it = iter(refs)
    r_ref, k_ref, v_ref, lo_ref = next(it), next(it), next(it), next(it)
    sm_ref, vf_ref = (next(it), next(it)) if has_vres else (None, None)
    mu_ref, mulo_ref, vec_ref, w2_ref, a2_ref, g2_ref = (next(it) for _ in range(6))
    musm_ref, v2_ref = (next(it), next(it)) if has_vres else (None, None)
    y_ref = next(it)
    vfo_ref = None if has_vres else next(it)
    s_ref, pr_ref, pk_ref, pv_ref, plo_ref = (next(it) for _ in range(5))
    psm_ref = next(it) if has_vres else None

    TB, W = r_ref.shape[1], r_ref.shape[2]
    C = min(RW_CHUNK, TB)
    P = W // LANES

    @pl.when(pl.program_id(1) == 0)
    def _():
        s_ref[...] = jnp.zeros_like(s_ref)
        for ref in (pr_ref, pk_ref, pv_ref, plo_ref, psm_ref):
            if ref is not None:
                ref[...] = jnp.zeros_like(ref)

    vec = vec_ref[...]
    w0, a0, k_k, k_a, ln_w, ln_b, r_k, v0 = (vec[i:i + 1] for i in range(8))
    mu = mu_ref[...]
    r = _token_shift(r_ref[0], mu[0:1], pr_ref)
    k = _token_shift(k_ref[0], mu[1:2], pk_ref)
    v = _token_shift(v_ref[0], mu[2:3], pv_ref)
    lo = _token_shift(lo_ref[0], mulo_ref[...], plo_ref)
    lo_wa, lo_g = lo[:, :LANES], lo[:, LANES:]

    w_raw = -_softplus(-(w0 + _dot(jnp.tanh(lo_wa).astype(BF16), w2_ref[...]))) - 0.5
    lw = -jnp.exp(w_raw)
    a = _gate_sigmoid(a0 + _dot(lo_wa.astype(BF16), a2_ref[...]))
    g = _dot(_gate_sigmoid(lo_g).astype(BF16), g2_ref[...])
    if has_vres:
        sm = _token_shift(sm_ref[0], musm_ref[...], psm_ref)
        v = v + (vf_ref[0] - v) * _gate_sigmoid(v0 + _dot(sm.astype(BF16), v2_ref[...]))
    else:
        vfo_ref[0] = v

    rr2 = lax.broadcasted_iota(jnp.int32, (2 * LANES, 2 * LANES), 0)
    cc2 = lax.broadcasted_iota(jnp.int32, (2 * LANES, 2 * LANES), 1)
    ones_bd = ((rr2 // RW_HEAD) == (cc2 // RW_HEAD)).astype(BF16)
    rr = lax.broadcasted_iota(jnp.int32, (LANES, LANES), 0)
    cc = lax.broadcasted_iota(jnp.int32, (LANES, LANES), 1)

    kk = k * k_k
    kmod = k * (1.0 + (a - 1.0) * k_a)
    stats = _head_sum(jnp.concatenate([kk * kk, r * kmod * r_k], axis=0), ones_bd)
    kk = kk / jnp.maximum(jnp.sqrt(stats[:TB]), 1e-12)
    bonus_dot = stats[TB:]
    ahat = kk * a

    tr = lax.broadcasted_iota(jnp.int32, (TB, TB), 0)
    tc = lax.broadcasted_iota(jnp.int32, (TB, TB), 1)
    cum = _cumsum_rows(((tr >= tc) & ((tr // C) == (tc // C))).astype(BF16), lw)
    e_inc = jnp.exp(cum)
    e_inv = jnp.exp(-cum)
    r_dec = r * e_inc
    kk_dec = kk * jnp.exp(cum - lw)
    a_inv = ahat * e_inv
    k_inv = kmod * e_inv

    lane = lax.broadcasted_iota(jnp.int32, (C, LANES), 1)
    first = lane < RW_HEAD

    def stack(zp):
        return jnp.concatenate([jnp.where(first, zp, 0.0), jnp.where(first, 0.0, zp)], axis=0).astype(BF16)

    strict = rr > cc
    incl = rr >= cc
    eye = (rr == cc).astype(F32)
    n_sq = int(np.log2(C)) - 1
    n2 = 2 * C
    units = [(j, p) for j in range(TB // C) for p in range(P)]
    cut = lambda t, u: t[u[0] * C:(u[0] + 1) * C, u[1] * LANES:(u[1] + 1) * LANES]
    kks = {u: stack(cut(kk_dec, u)) for u in units}
    rs = {u: stack(cut(r_dec, u)) for u in units}
    vs = {u: stack(cut(v, u)) for u in units}
    ak = {u: jnp.concatenate([stack(cut(a_inv, u)), stack(cut(k_inv, u))], axis=0) for u in units}
    sc = {u: _dot_t(jnp.concatenate([kks[u], rs[u]], axis=0), ak[u]) for u in units}
    l_a = {u: jnp.where(strict, sc[u][:n2, :n2], 0.0) for u in units}
    lkv = {u: _dot(jnp.where(strict, sc[u][:n2, n2:], 0.0).astype(BF16), vs[u]) for u in units}
    a_cat = {u: jnp.where(jnp.concatenate([incl, incl], axis=1), sc[u][n2:], 0.0).astype(BF16) for u in units}
    t_inv = {u: eye - l_a[u] for u in units}
    pw = {u: _dot(l_a[u].astype(BF16), l_a[u].astype(BF16)).astype(BF16) for u in units}
    for _ in range(n_sq - 1):
        both = {u: _dot(pw[u], jnp.concatenate([t_inv[u].astype(BF16), pw[u]], axis=1)) for u in units}
        t_inv = {u: t_inv[u] + both[u][:, :n2] for u in units}
        pw = {u: both[u][:, n2:].astype(BF16) for u in units}
    t_inv = {u: t_inv[u] + _dot(pw[u], t_inv[u].astype(BF16)) for u in units}
    rhs = {u: jnp.concatenate([-kks[u].astype(F32), -lkv[u]], axis=1).astype(BF16) for u in units}
    wv = {u: _dot(t_inv[u].astype(BF16), rhs[u]) for u in units}

    st = [s_ref[p] for p in range(P)]
    y_rows = []
    for j in range(TB // C):
        us = [(j, p) for p in range(P)]
        wr = [_dot_t(jnp.concatenate([wv[u][:, :LANES].astype(BF16), rs[u]], axis=0), st[u[1]].astype(BF16))
              for u in us]
        uv = [jnp.concatenate([(wr[p][:n2] + wv[u][:, LANES:]).astype(BF16), vs[u]], axis=0)
              for p, u in enumerate(us)]
        yst = [wr[p][n2:] + _dot(a_cat[u], uv[p]) for p, u in enumerate(us)]
        gamma_end = e_inc[(j + 1) * C - 1:(j + 1) * C]
        st = [(st[p] + _tdot(uv[p], ak[u])) * gamma_end[:, p * LANES:(p + 1) * LANES]
              for p, u in enumerate(us)]
        y_rows.append(jnp.concatenate([yst[p][:C] + yst[p][C:] for p in range(P)], axis=1))
    for p in range(P):
        s_ref[p] = st[p]
    y = jnp.concatenate(y_rows, axis=0)

    inv_n = 1.0 / RW_HEAD
    yc = y - _head_sum(y, ones_bd, two_pass=True) * inv_n
    yn = yc * lax.rsqrt(_head_sum(yc * yc, ones_bd) * inv_n + RW_LN_EPS) * ln_w + ln_b
    y_ref[0] = (yn + bonus_dot * v) * g


def _rwkv(z3, v_first, prm, *, rkv_col0, lora_col0, small_col0, width):
    B, T, _ = z3.shape
    C = min(RW_BLOCK, T)
    has_vres = v_first is not None
    lora_w = RW_DECAY_LORA + RW_AAA_LORA + RW_GATE_LORA
    tok = lambda wdt, col0, j=0: pl.BlockSpec((1, C, wdt), lambda b, t: (b, t, col0 // wdt + j))
    full = lambda arr: pl.BlockSpec(arr.shape, lambda b, t: (0,) * arr.ndim)
    tok_out = pl.BlockSpec((1, C, width), lambda b, t: (b, t, 0))

    args = [z3, z3, z3, z3]
    specs = [tok(width, rkv_col0, 0), tok(width, rkv_col0, 1), tok(width, rkv_col0, 2), tok(lora_w, lora_col0)]
    if has_vres:
        args += [z3, v_first]
        specs += [tok(LANES, small_col0), tok_out]
    weights = [prm["mu_rkv"], prm["mu_lora"], prm["vecs"], prm["w2"], prm["a2"], prm["g2"]]
    if has_vres:
        weights += [prm["mu_small"], prm["v2"]]
    args += weights
    specs += [full(wt) for wt in weights]

    y_shape = jax.ShapeDtypeStruct((B, T, width), F32)
    scratch = [pltpu.VMEM((width // LANES, LANES, LANES), F32)]
    scratch += [pltpu.VMEM((8, width), F32)] * 3 + [pltpu.VMEM((8, lora_w), F32)]
    if has_vres:
        scratch += [pltpu.VMEM((8, LANES), F32)]
    out = pl.pallas_call(
        functools.partial(_rwkv_kernel, has_vres=has_vres),
        out_shape=y_shape if has_vres else (y_shape, y_shape),
        grid=(B, T // C),
        in_specs=specs,
        out_specs=tok_out if has_vres else (tok_out, tok_out),
        scratch_shapes=scratch,
        compiler_params=_cparams(("arbitrary", "arbitrary")),
        name="rwkv7",
    )(*args)
    return (out, v_first) if has_vres else out


def _pad_rows(w, row0, rows):
    return jnp.zeros((rows, w.shape[1]), w.dtype).at[row0:row0 + w.shape[0]].set(w)


def kernel(x, c, norm_w, final_norm_w, ada_w, ada_b, ffn_up, ffn_down, w_in, w_in_vres,
           ml_conv_w, ml_conv_b, ml_wq, ml_wk, ml_i_b, ml_f_b, ml_norm_w, ml_skip,
           rw_mu, rw_mu_vres, rw_w0, rw_w2, rw_a0, rw_a2, rw_v0, rw_v2, rw_g2,
           rw_k_k, rw_k_a, rw_r_k, rw_ln_w, rw_ln_b, hg_lb_logits, hg_norm_w,
           w_branch, w_out):
    B, T, D = x.shape
    depth = norm_w.shape[0]
    N = B * T
    W = D
    lora_w = RW_DECAY_LORA + RW_AAA_LORA + RW_GATE_LORA

    o_mlx, o_mlo, o_mli, o_mlf = 0, W, 2 * W, 2 * W + ML_HEADS
    o_rw = 2 * W + 2 * ML_HEADS
    o_hg = o_rw + 3 * W + lora_w
    o_gate = o_hg + 4 * W
    n_in = o_gate + N_BRANCH * D
    c_rkv, c_hg = 2 * W, 5 * W
    c_lora = c_hg + 4 * W
    c_small = c_lora + lora_w
    n_cols = c_small + LANES
    s_vlo = 2 * ML_HEADS

    lb_soft = jax.nn.softmax(hg_lb_logits.astype(F32), axis=0)
    lower_bounds = jnp.cumsum(lb_soft, axis=0) - lb_soft[0]

    mod_all = _ada_mod(c, ada_w, ada_b).reshape(depth, B, N_SUB * 3, D)
    nw3 = norm_w.reshape(depth * N_SUB, 1, D)
    ffn_up_b, ffn_down_b = ffn_up.astype(BF16), ffn_down.astype(BF16)
    w_branch_b, w_out_b = w_branch.astype(BF16), w_out.astype(BF16)
    x2 = x.reshape(N, D)
    v_first = None
    for l in range(depth):
        wl = w_in[l]
        small = jnp.concatenate([wl[:, o_mli:o_mli + 2 * ML_HEADS]]
                                + ([w_in_vres[l - 1]] if l > 0 else []), axis=1)
        small = jnp.pad(small, ((0, 0), (0, LANES - small.shape[1])))
        w_cat = jnp.concatenate([
            wl[:, o_mlx:o_mlx + 2 * W], wl[:, o_rw:o_rw + 3 * W], wl[:, o_hg:o_hg + 4 * W],
            wl[:, o_rw + 3 * W:o_rw + 3 * W + lora_w], small], axis=1).astype(BF16)
        w_cat = w_cat.reshape(D, n_cols // INP_TN, INP_TN).transpose(1, 0, 2)

        x2 = _ffn(x2, mod_all, nw3, ffn_up_b, ffn_down_b, final_norm_w,
                  layer=l, half=0, tokens_per_batch=T, final_norm=False)

        z = _inproj(x2, mod_all, nw3, w_cat, layer=l, tokens_per_batch=T)
        z3 = z.reshape(B, T, n_cols)

        gates = z3[:, :, c_small:c_small + 2 * ML_HEADS].transpose(0, 2, 1)
        y_ml = _mlstm(z3, gates, jnp.concatenate([ml_i_b[l], ml_f_b[l]]), ml_conv_w[l], ml_conv_b[l],
                      ml_wq[l], ml_wk[l], ml_norm_w[l], ml_skip[l], width=W)

        mu = rw_mu[l]
        prm = {
            "mu_rkv": mu[:3 * W].reshape(3, W),
            "mu_lora": mu[3 * W:].reshape(1, lora_w),
            "vecs": jnp.stack([rw_w0[l], rw_a0[l], rw_k_k[l], rw_k_a[l], rw_ln_w[l], rw_ln_b[l],
                               rw_r_k[l].reshape(W), rw_v0[l - 1] if l > 0 else jnp.zeros((W,), F32)]),
            "w2": _pad_rows(rw_w2[l], 0, LANES).astype(BF16),
            "a2": _pad_rows(rw_a2[l], RW_DECAY_LORA, LANES).astype(BF16),
            "g2": rw_g2[l].astype(BF16),
        }
        if l > 0:
            prm["mu_small"] = jnp.zeros((1, LANES), F32).at[0, s_vlo:s_vlo + RW_MV_LORA].set(rw_mu_vres[l - 1])
            prm["v2"] = _pad_rows(rw_v2[l - 1], s_vlo, LANES).astype(BF16)
        y_rw, v_first = _rwkv(z3, v_first, prm, rkv_col0=c_rkv, lora_col0=c_lora,
                              small_col0=c_small, width=W)

        y_hg = _hgrn(z3, lower_bounds[l], hg_norm_w[l], col0=c_hg, width=W)

        x2 = _branch_mix(x2, mod_all, nw3, y_ml.reshape(N, W), y_rw.reshape(N, W), y_hg.reshape(N, W),
                         wl[:, o_gate:n_in].astype(BF16), w_branch_b, w_out_b, layer=l, tokens_per_batch=T)

        x2 = _ffn(x2, mod_all, nw3, ffn_up_b, ffn_down_b, final_norm_w,
                  layer=l, half=1, tokens_per_batch=T, final_norm=(l == depth - 1))
    return x2.reshape(B, T, D)
```

```python
import functools

import numpy as np
import jax
import jax.numpy as jnp
from jax import lax
from jax.experimental import pallas as pl
from jax.experimental.pallas import tpu as pltpu

F32 = jnp.float32
BF16 = jnp.bfloat16

EPS = 1e-6
NEG_BIG = -1e30
TINY = 1e-30

N_SUB = 3
N_BRANCH = 3
ML_HEADS = 4
ML_CONV = 4
RW_HEAD = 64
RW_LN_EPS = 64e-5
RW_DECAY_LORA = 64
RW_AAA_LORA = 64
RW_MV_LORA = 32
RW_GATE_LORA = 128
HG_EXPAND = 128

LANES = 128
VMEM_LIMIT = 56 * 1024 * 1024

FFN_TM = 512
FFN_FC = 256
INP_TM = 1024
INP_TN = 1920
MIX_TM = 512
ML_CHUNK = 256
RW_CHUNK = 64
HG_CHUNK = 128
MIXER_ROWS = 256


def _cparams(sem):
    return pltpu.CompilerParams(dimension_semantics=sem, vmem_limit_bytes=VMEM_LIMIT)


def _dot(a, b):
    return jnp.dot(a, b, preferred_element_type=F32)


def _dot_t(a, b):
    return lax.dot_general(a, b, (((1,), (1,)), ((), ())), preferred_element_type=F32)


def _tdot(a, b):
    return lax.dot_general(a, b, (((0,), (0,)), ((), ())), preferred_element_type=F32)


def _split3(x):
    hi = x.astype(BF16)
    r1 = x - hi.astype(F32)
    mid = r1.astype(BF16)
    lo = (r1 - mid.astype(F32)).astype(BF16)
    return hi, mid, lo


def _cumsum_rows(tri, x):
    hi, mid, lo = _split3(x)
    return _dot(tri, hi) + (_dot(tri, mid) + _dot(tri, lo))


def _cumsum_lanes(x, tri):
    hi, mid, lo = _split3(x)
    return _dot(hi, tri) + (_dot(mid, tri) + _dot(lo, tri))


def _sigmoid(x):
    return 1.0 / (1.0 + jnp.exp(-x))


def _gate_sigmoid(x):
    return 0.5 * jnp.tanh(0.5 * x) + 0.5


def _silu(x):
    return x * _gate_sigmoid(x)


def _modulated_norm(x, nw, shift, scale):
    ms = jnp.mean(x * x, axis=-1, keepdims=True)
    return (x * lax.rsqrt(ms + EPS) * nw) * (1.0 + scale) + shift


def _ada_kernel(c_ref, w_ref, b_ref, o_ref):
    cond = _silu(c_ref[...])
    o_ref[0] = _dot(cond, w_ref[0]) + b_ref[0]


def _ada_mod(c, ada_w, ada_b):
    L, D, M = ada_w.shape
    B = c.shape[0]
    tn = M // 8
    return pl.pallas_call(
        _ada_kernel,
        out_shape=jax.ShapeDtypeStruct((L, B, M), F32),
        grid=(L, M // tn),
        in_specs=[
            pl.BlockSpec((B, D), lambda l, j: (0, 0)),
            pl.BlockSpec((1, D, tn), lambda l, j: (l, 0, j)),
            pl.BlockSpec((1, 1, tn), lambda l, j: (l, 0, j)),
        ],
        out_specs=pl.BlockSpec((1, B, tn), lambda l, j: (l, 0, j)),
        compiler_params=_cparams(("arbitrary", "arbitrary")),
        name="ada_mod",
    )(c, ada_w, ada_b.reshape(L, 1, M))


def _ffn_kernel(x_ref, mod_ref, nw_ref, wu_ref, wd_ref, fw_ref, o_ref, acc_ref, *, sub, final_norm):
    x = x_ref[...]
    mod = mod_ref[0, 0]
    shift, scale, gate = (mod[3 * sub + i:3 * sub + i + 1] for i in range(3))
    hb = _modulated_norm(x, nw_ref[0], shift, scale).astype(BF16)
    F = wd_ref.shape[2]
    for c in range(F // FFN_FC):
        cols = slice(c * FFN_FC, (c + 1) * FFN_FC)
        a = _dot(hb, wu_ref[0, 0, :, cols])
        b = _dot(hb, wu_ref[0, 0, :, F + c * FFN_FC:F + (c + 1) * FFN_FC])
        act = (_silu(a) * b).astype(BF16)
        d = _dot(act, wd_ref[0, 0, cols, :])
        if c == 0:
            acc_ref[...] = d
        else:
            acc_ref[...] += d
    y = x + (0.5 * (1.0 + gate)) * acc_ref[...]
    if final_norm:
        ms = jnp.mean(y * y, axis=-1, keepdims=True)
        y = y * lax.rsqrt(ms + EPS) * fw_ref[...]
    o_ref[...] = y


def _ffn(x, mod_all, norm_w, w_up, w_down, fw, *, layer, half, tokens_per_batch, final_norm):
    N, D = x.shape
    sub = 2 * half
    bpb = tokens_per_batch // FFN_TM
    return pl.pallas_call(
        functools.partial(_ffn_kernel, sub=sub, final_norm=final_norm),
        out_shape=jax.ShapeDtypeStruct((N, D), F32),
        grid=(N // FFN_TM,),
        in_specs=[
            pl.BlockSpec((FFN_TM, D), lambda i: (i, 0)),
            pl.BlockSpec((1, 1) + mod_all.shape[2:], lambda i: (layer, i // bpb, 0, 0)),
            pl.BlockSpec((1, 1, D), lambda i: (N_SUB * layer + sub, 0, 0)),
            pl.BlockSpec((1, 1) + w_up.shape[2:], lambda i: (layer, half, 0, 0), pipeline_mode=pl.Buffered(1)),
            pl.BlockSpec((1, 1) + w_down.shape[2:], lambda i: (layer, half, 0, 0), pipeline_mode=pl.Buffered(1)),
            pl.BlockSpec((1, D), lambda i: (0, 0)),
        ],
        out_specs=pl.BlockSpec((FFN_TM, D), lambda i: (i, 0)),
        scratch_shapes=[pltpu.VMEM((FFN_TM, D), F32)],
        compiler_params=_cparams(("arbitrary",)),
        name="ffn_half",
    )(x, mod_all, norm_w, w_up, w_down, fw.reshape(1, D))


def _inproj_kernel(x_ref, mod_ref, nw_ref, w_ref, o_ref, h_ref, *, sub):
    @pl.when(pl.program_id(1) == 0)
    def _():
        mod = mod_ref[0, 0]
        shift, scale = mod[3 * sub:3 * sub + 1], mod[3 * sub + 1:3 * sub + 2]
        h_ref[...] = _modulated_norm(x_ref[...], nw_ref[0], shift, scale).astype(BF16)

    o_ref[...] = _dot(h_ref[...], w_ref[pl.program_id(1)])


def _inproj(x, mod_all, norm_w, w, *, layer, tokens_per_batch):
    N, D = x.shape
    n_tiles = w.shape[0]
    sub = 1
    tm = min(INP_TM, tokens_per_batch)
    bpb = tokens_per_batch // tm
    return pl.pallas_call(
        functools.partial(_inproj_kernel, sub=sub),
        out_shape=jax.ShapeDtypeStruct((N, n_tiles * INP_TN), F32),
        grid=(N // tm, n_tiles),
        in_specs=[
            pl.BlockSpec((tm, D), lambda i, j: (i, 0)),
            pl.BlockSpec((1, 1) + mod_all.shape[2:], lambda i, j: (layer, i // bpb, 0, 0)),
            pl.BlockSpec((1, 1, D), lambda i, j: (N_SUB * layer + sub, 0, 0)),
            pl.BlockSpec(w.shape, lambda i, j: (0, 0, 0), pipeline_mode=pl.Buffered(1)),
        ],
        out_specs=pl.BlockSpec((tm, INP_TN), lambda i, j: (i, j)),
        scratch_shapes=[pltpu.VMEM((tm, D), BF16)],
        compiler_params=_cparams(("arbitrary", "arbitrary")),
        name="mixer_inproj",
    )(x, mod_all, norm_w, w)


def _mix_kernel(x_ref, mod_ref, nw_ref, yml_ref, yrw_ref, yhg_ref, wg_ref, wbr_ref, wout_ref, o_ref,
                *, sub):
    x = x_ref[...]
    D = x.shape[1]
    mod = mod_ref[0, 0]
    shift, scale, gate = (mod[3 * sub + i:3 * sub + i + 1] for i in range(3))
    hb = _modulated_norm(x, nw_ref[0], shift, scale).astype(BF16)
    mixed = None
    for n, y_ref in enumerate((yml_ref, yrw_ref, yhg_ref)):
        gate_pre = _dot(hb, wg_ref[:, n * D:(n + 1) * D])
        term = _gate_sigmoid(gate_pre) * _dot(y_ref[...].astype(BF16), wbr_ref[0, n])
        mixed = term if mixed is None else mixed + term
    o_ref[...] = x + (1.0 + gate) * _dot(mixed.astype(BF16), wout_ref[0])


def _branch_mix(x, mod_all, norm_w, y_ml, y_rw, y_hg, w_gate, w_branch, w_out, *, layer, tokens_per_batch):
    N, D = x.shape
    sub = 1
    bpb = tokens_per_batch // MIX_TM
    tok = pl.BlockSpec((MIX_TM, D), lambda i: (i, 0))
    return pl.pallas_call(
        functools.partial(_mix_kernel, sub=sub),
        out_shape=jax.ShapeDtypeStruct((N, D), F32),
        grid=(N // MIX_TM,),
        in_specs=[
            tok,
            pl.BlockSpec((1, 1) + mod_all.shape[2:], lambda i: (layer, i // bpb, 0, 0)),
            pl.BlockSpec((1, 1, D), lambda i: (N_SUB * layer + sub, 0, 0)),
            tok, tok, tok,
            pl.BlockSpec(w_gate.shape, lambda i: (0, 0), pipeline_mode=pl.Buffered(1)),
            pl.BlockSpec((1,) + w_branch.shape[1:], lambda i: (layer, 0, 0, 0), pipeline_mode=pl.Buffered(1)),
            pl.BlockSpec((1,) + w_out.shape[1:], lambda i: (layer, 0, 0), pipeline_mode=pl.Buffered(1)),
        ],
        out_specs=tok,
        compiler_params=_cparams(("arbitrary",)),
        name="branch_mix",
    )(x, mod_all, norm_w, y_ml, y_rw, y_hg, w_gate, w_branch, w_out)


def _log_sigmoid(x):
    return jnp.minimum(x, 0.0) - jnp.log1p(jnp.exp(-jnp.abs(x)))


def _mlstm_kernel(bias_ref, x_ref, op_ref, g_ref, cw_ref, cb_ref, wq_ref, wk_ref,
                  nw_ref, sk_ref, y_ref, c_ref, m_ref, tail_ref):
    TB, W = x_ref.shape[1], x_ref.shape[2]
    L = min(ML_CHUNK, TB)
    H = ML_HEADS
    dh = W // H
    heads = range(H)
    chunks = range(TB // L)
    sls = [slice(h * dh, (h + 1) * dh) for h in heads]
    rws = [slice(j * L, (j + 1) * L) for j in chunks]

    x = x_ref[0]
    cw = cw_ref[...]
    tail = tail_ref[...]
    row8 = lax.broadcasted_iota(jnp.int32, (8, W), 0)
    conv = x * cw[ML_CONV - 1:ML_CONV]
    for s in range(1, ML_CONV):
        xr = pltpu.roll(x, s, 0)
        top = jnp.where(row8 < s, pltpu.roll(tail, s, 0), xr[0:8])
        xs = jnp.concatenate([top, xr[8:]], axis=0)
        conv = conv + xs * cw[ML_CONV - 1 - s:ML_CONV - s]
    tail_ref[...] = x[TB - 8:TB]
    xc = _silu(conv + cb_ref[...])
    xcb = xc.astype(BF16)
    vb = x.astype(BF16)
    yield 0.2

    g = g_ref[0]
    li_all = jnp.concatenate([g[h:h + 1] + bias_ref[h] for h in heads], axis=0)
    lf_all = _log_sigmoid(jnp.concatenate([g[H + h:H + h + 1] + bias_ref[H + h] for h in heads], axis=0))
    r = lax.broadcasted_iota(jnp.int32, (L, L), 0)
    c = lax.broadcasted_iota(jnp.int32, (L, L), 1)
    causal = r >= c
    tri = (r <= c).astype(BF16)

    lane = lax.broadcasted_iota(jnp.int32, (H, L), 1)
    m_prev = m_ref[...][:, 0:1]
    g_rows, decay, col_rows = [], [], []
    for rw in rws:
        li = li_all[:, rw]
        b_rows = _cumsum_lanes(lf_all[:, rw], tri)
        g_rows.append(li - b_rows)
        cm = g_rows[-1]
        sh = 1
        while sh < L:
            cm = jnp.maximum(cm, jnp.where(lane >= sh, pltpu.roll(cm, sh, 1), NEG_BIG))
            sh *= 2
        mx = jnp.maximum(m_prev, cm)
        b_end = b_rows[:, L - 1:L]
        log_ws = b_end - b_rows + li
        m_new = jnp.maximum(b_end + m_prev, jnp.max(log_ws, axis=-1, keepdims=True))
        decay.append(jnp.exp(b_end + m_prev - m_new))
        col_rows += [-mx, jnp.exp(m_prev - mx), jnp.exp(-(b_rows + mx)), jnp.exp(log_ws - m_new)]
        m_prev = m_new
    m_ref[...] = jnp.broadcast_to(m_prev, (H, LANES))
    cols = jnp.concatenate(col_rows + [jnp.zeros((LANES - 4 * H * len(rws), L), F32)], axis=0).T

    ones_col = (lax.broadcasted_iota(jnp.int32, (TB, LANES), 1) == 0).astype(BF16)
    vaug = [jnp.concatenate([vb[:, sl], ones_col], axis=1) for sl in sls]

    yield 0.3
    q = [_dot(xcb[:, sl], wq_ref[h]) for h, sl in zip(heads, sls)]
    k = [_dot(xcb[:, sl], wk_ref[h]) * (dh ** -0.5) for h, sl in zip(heads, sls)]
    qb = [t.astype(BF16) for t in q]
    kb = [t.astype(BF16) for t in k]
    yield 0.4
    units = [(j, h) for j in chunks for h in heads]
    col = lambda u, i: cols[:, (4 * u[0] + i) * H + u[1]:(4 * u[0] + i) * H + u[1] + 1]
    qk = {u: _dot_t(qb[u[1]][rws[u[0]]], kb[u[1]][rws[u[0]]]) for u in units}
    sv, kw = {}, {}
    for n, u in enumerate(units):
        s_mat = qk[u] * jnp.where(causal, jnp.exp(col(u, 0) + g_rows[u[0]][u[1]:u[1] + 1]), 0.0)
        sv[u] = _dot(s_mat.astype(BF16), vaug[u[1]][rws[u[0]]])
        kw[u] = (k[u[1]][rws[u[0]]] * col(u, 3)).astype(BF16)
        yield 0.4 + 0.35 * (n + 1) / len(units)

    state = [c_ref[h] for h in heads]
    hs = {}
    for j in chunks:
        qc = [_dot(qb[h][rws[j]], state[h].astype(BF16)) for h in heads]
        for h in heads:
            u = (j, h)
            both = col(u, 1) * qc[h] + sv[u]
            inv = 1.0 / jnp.maximum(jnp.abs(both[:, dh:dh + 1]), col(u, 2))
            hs[u] = both[:, :dh] * inv
        state = [decay[j][h:h + 1] * state[h] + _tdot(kw[(j, h)], vaug[h][rws[j]]) for h in heads]
    for h in heads:
        c_ref[h] = state[h]
    yield 0.85

    gate = _gate_sigmoid(op_ref[0])
    nw, sk = nw_ref[...], sk_ref[...]
    for h, sl in zip(heads, sls):
        hg = jnp.concatenate([hs[(j, h)] for j in chunks], axis=0) * gate[:, sl]
        hg = hg - jnp.mean(hg, axis=-1, keepdims=True)
        hn = hg * lax.rsqrt(jnp.mean(hg * hg, axis=-1, keepdims=True) + EPS)
        y_ref[0, :, sl] = hn * nw[:, sl] + sk[:, sl] * xc[:, sl]
        yield 0.85 + 0.15 * (h + 1) / H


def _mlstm_plan(z3, gates, bias, conv_w, conv_b, wq, wk, norm_w, skip, *, width, rows):
    B, T, _ = z3.shape
    dh = width // ML_HEADS
    vec = lambda b, t: (0, 0)
    return dict(
        body=_mlstm_kernel,
        args=[bias, z3, z3, gates, conv_w, conv_b.reshape(1, width), wq.astype(BF16),
              wk.astype(BF16), norm_w.reshape(1, width), skip.reshape(1, width)],
        in_specs=[
            pl.BlockSpec(memory_space=pltpu.SMEM),
            pl.BlockSpec((1, rows, width), lambda b, t: (b, t, 0)),
            pl.BlockSpec((1, rows, width), lambda b, t: (b, t, 1)),
            pl.BlockSpec((1, 2 * ML_HEADS, rows), lambda b, t: (b, 0, t)),
            pl.BlockSpec((ML_CONV, width), vec),
            pl.BlockSpec((1, width), vec),
            pl.BlockSpec((ML_HEADS, dh, dh), lambda b, t: (0, 0, 0)),
            pl.BlockSpec((ML_HEADS, dh, dh), lambda b, t: (0, 0, 0)),
            pl.BlockSpec((1, width), vec),
            pl.BlockSpec((1, width), vec),
        ],
        out_shape=[jax.ShapeDtypeStruct((B, T, width), F32)],
        out_specs=[pl.BlockSpec((1, rows, width), lambda b, t: (b, t, 0))],
        scratch=[
            pltpu.VMEM((ML_HEADS, dh, dh + LANES), F32),
            pltpu.VMEM((ML_HEADS, LANES), F32),
            pltpu.VMEM((8, width), F32),
        ],
    )


def _zero_refs(refs):
    for ref in refs:
        ref[...] = jnp.zeros_like(ref)


def _plans_kernel(*refs, plans):
    n_in = [len(p["args"]) for p in plans]
    n_out = [len(p["out_shape"]) for p in plans]
    n_scr = [len(p["scratch"]) for p in plans]
    ins, outs, scr = refs[:sum(n_in)], refs[sum(n_in):sum(n_in) + sum(n_out)], refs[sum(n_in) + sum(n_out):]
    pl.when(pl.program_id(1) == 0)(functools.partial(_zero_refs, scr))
    i = o = s = 0
    for p, ni, no, ns in zip(plans, n_in, n_out, n_scr):
        for _ in p["body"](*ins[i:i + ni], *outs[o:o + no], *scr[s:s + ns]):
            pass
        i, o, s = i + ni, o + no, s + ns


def _run_plans(plans, *, batch, steps, name):
    static = [dict(body=p["body"], args=[None] * len(p["args"]), out_shape=[None] * len(p["out_shape"]),
                   scratch=[None] * len(p["scratch"])) for p in plans]
    return pl.pallas_call(
        functools.partial(_plans_kernel, plans=static),
        out_shape=[s for p in plans for s in p["out_shape"]],
        grid=(batch, steps),
        in_specs=[s for p in plans for s in p["in_specs"]],
        out_specs=[s for p in plans for s in p["out_specs"]],
        scratch_shapes=[s for p in plans for s in p["scratch"]],
        compiler_params=_cparams(("arbitrary", "arbitrary")),
        name=name,
    )(*[a for p in plans for a in p["args"]])


def _midrow_broadcast(b, m):
    n, w = b.shape
    row = lambda i: b[i:i + 1]
    if m >= 8:
        return jnp.concatenate(
            [jnp.broadcast_to(row(p * 2 * m + m), (2 * m, w)) for p in range(n // (2 * m))], axis=0)
    pick = lambda off: jnp.concatenate(
        [jnp.broadcast_to(row(8 * g + off), (8, w)) for g in range(n // 8)], axis=0)
    sub = lax.broadcasted_iota(jnp.int32, (n, w), 0) % 8
    if m == 4:
        return pick(4)
    if m == 2:
        return jnp.where(sub < 4, pick(2), pick(6))
    return jnp.where(sub < 2, pick(1), jnp.where(sub < 4, pick(3), jnp.where(sub < 6, pick(5), pick(7))))


def _hgrn_kernel(q_ref, f_ref, i_ref, g_ref, lb_ref, nw_ref, y_ref, s_ref):
    TB, W = q_ref.shape[1], q_ref.shape[2]
    C = min(HG_CHUNK, TB)
    H = W // HG_EXPAND
    chunks = range(TB // C)
    rws = [slice(j * C, (j + 1) * C) for j in chunks]

    lb = lb_ref[...]
    sig = _sigmoid(f_ref[0])
    q = _silu(q_ref[0])
    k = (1.0 - lb) * (1.0 - sig)
    log_f = jnp.log(jnp.maximum(lb + (1.0 - lb) * sig, TINY))
    rt = lax.broadcasted_iota(jnp.int32, (TB, TB), 0)
    ct = lax.broadcasted_iota(jnp.int32, (TB, TB), 1)
    b = _cumsum_rows(((rt >= ct) & ((rt // C) == (ct // C))).astype(BF16), log_f)
    v = i_ref[0]
    vb = v.astype(BF16)
    yield 0.1

    r = lax.broadcasted_iota(jnp.int32, (C, C), 0)
    c = lax.broadcasted_iota(jnp.int32, (C, C), 1)
    rowi = lax.broadcasted_iota(jnp.int32, (TB, W), 0)
    qb, kb = q.astype(BF16), k.astype(BF16)
    heads = range(H)
    sls = [slice(h * HG_EXPAND, (h + 1) * HG_EXPAND) for h in heads]
    units = [(j, h) for j in chunks for h in heads]
    a = {u: jnp.zeros((C, C), F32) for u in units}
    n_levels = int(np.log2(C))
    m = 1
    while m < C:
        e = jnp.exp(-jnp.abs(b - _midrow_broadcast(b, m)))
        z_l = jnp.where((rowi % (2 * m)) >= m, qb, kb) * e.astype(BF16)
        mask = ((r // (2 * m)) == (c // (2 * m))) & ((r % (2 * m)) >= m) & ((c % (2 * m)) < m)
        for j, h in units:
            zu = z_l[rws[j], sls[h]]
            a[(j, h)] = jnp.where(mask, _dot_t(zu, zu), a[(j, h)])
        m *= 2
        yield 0.1 + 0.6 * int(np.log2(m)) / n_levels

    qe = (q * jnp.exp(b)).astype(BF16)
    b_ends = [b[(j + 1) * C - 1:(j + 1) * C] for j in chunks]
    b_end_rows = jnp.concatenate([jnp.broadcast_to(b_ends[j], (C, W)) for j in chunks], axis=0)
    kd = (k * jnp.exp(b_end_rows - b)).astype(BF16)
    diag = q * k
    intra = {(j, h): _dot(a[(j, h)].astype(BF16), vb[rws[j], sls[h]])
             + jnp.sum(diag[rws[j], sls[h]], axis=-1, keepdims=True) * v[rws[j], sls[h]] for j, h in units}
    kv = {(j, h): _tdot(vb[rws[j], sls[h]], kd[rws[j], sls[h]]) for j, h in units}
    yield 0.8
    st = [s_ref[h] for h in heads]
    o_rows = []
    for j in chunks:
        inter = [_dot_t(qe[rws[j], sls[h]], st[h].astype(BF16)) for h in heads]
        g_end = jnp.exp(b_ends[j])
        st = [st[h] * g_end[:, sls[h]] + kv[(j, h)] for h in heads]
        o_rows.append([inter[h] + intra[(j, h)] for h in heads])
    for h in heads:
        s_ref[h] = st[h]
    yield 0.9
    gate = _gate_sigmoid(g_ref[0])
    nw = nw_ref[...]
    for h, sl in zip(heads, sls):
        o = jnp.concatenate([o_rows[j][h] for j in chunks], axis=0)
        on = o * lax.rsqrt(jnp.mean(o * o, axis=-1, keepdims=True) + EPS)
        y_ref[0, :, sl] = on * nw[:, sl] * gate[:, sl]
    yield 1.0


def _hgrn_plan(z3, lower_bound, norm_w, *, col0, width, rows):
    B, T, _ = z3.shape
    H = width // HG_EXPAND
    blk0 = col0 // width
    spec = lambda j: pl.BlockSpec((1, rows, width), lambda b, t: (b, t, blk0 + j))
    vec = pl.BlockSpec((1, width), lambda b, t: (0, 0))
    return dict(
        body=_hgrn_kernel,
        args=[z3, z3, z3, z3, lower_bound.reshape(1, width), norm_w.reshape(1, width)],
        in_specs=[spec(0), spec(1), spec(2), spec(3), vec, vec],
        out_shape=[jax.ShapeDtypeStruct((B, T, width), F32)],
        out_specs=[pl.BlockSpec((1, rows, width), lambda b, t: (b, t, 0))],
        scratch=[pltpu.VMEM((H, HG_EXPAND, HG_EXPAND), F32)],
    )


def _token_shift(z, mu, last_ref):
    n = z.shape[0]
    row = lax.broadcasted_iota(jnp.int32, z.shape, 0)
    prev = jnp.where(row == 0, last_ref[7:8], pltpu.roll(z, 1, 0))
    last_ref[...] = z[n - 8:n]
    return z + mu * (prev - z)


def _head_sum(x, ones_bd, two_pass=False):
    n, w = x.shape
    slab = ones_bd.shape[0]
    xs = jnp.concatenate([x[:, g * slab:(g + 1) * slab] for g in range(w // slab)], axis=0)
    hi = xs.astype(BF16)
    s = _dot(hi, ones_bd)
    if two_pass:
        s = s + _dot((xs - hi.astype(F32)).astype(BF16), ones_bd)
    return jnp.concatenate([s[g * n:(g + 1) * n] for g in range(w // slab)], axis=1)


def _softplus(x):
    return jnp.maximum(x, 0.0) + jnp.log1p(jnp.exp(-jnp.abs(x)))


def _rwkv_kernel(*refs, has_vres):
    it = iter(refs)
    r_ref, k_ref, v_ref, lo_ref = next(it), next(it), next(it), next(it)
    sm_ref, vf_ref = (next(it), next(it)) if has_vres else (None, None)
    mu_ref, mulo_ref, vec_ref, w2_ref, a2_ref, g2_ref = (next(it) for _ in range(6))
    musm_ref, v2_ref = (next(it), next(it)) if has_vres else (None, None)
    y_ref = next(it)
    vfo_ref = None if has_vres else next(it)
    s_ref, pr_ref, pk_ref, pv_ref, plo_ref = (next(it) for _ in range(5))
    psm_ref = next(it) if has_vres else None

    TB, W = r_ref.shape[1], r_ref.shape[2]
    C = min(RW_CHUNK, TB)
    P = W // LANES

    vec = vec_ref[...]
    w0, a0, k_k, k_a, ln_w, ln_b, r_k, v0 = (vec[i:i + 1] for i in range(8))
    mu = mu_ref[...]
    r = _token_shift(r_ref[0], mu[0:1], pr_ref)
    k = _token_shift(k_ref[0], mu[1:2], pk_ref)
    v = _token_shift(v_ref[0], mu[2:3], pv_ref)
    lo = _token_shift(lo_ref[0], mulo_ref[...], plo_ref)
    lo_wa, lo_g = lo[:, :LANES], lo[:, LANES:]

    w_raw = -_softplus(-(w0 + _dot(jnp.tanh(lo_wa).astype(BF16), w2_ref[...]))) - 0.5
    lw = -jnp.exp(w_raw)
    a = _gate_sigmoid(a0 + _dot(lo_wa.astype(BF16), a2_ref[...]))
    g = _dot(_gate_sigmoid(lo_g).astype(BF16), g2_ref[...])
    if has_vres:
        sm = _token_shift(sm_ref[0], musm_ref[...], psm_ref)
        v = v + (vf_ref[0] - v) * _gate_sigmoid(v0 + _dot(sm.astype(BF16), v2_ref[...]))
    else:
        vfo_ref[0] = v
    yield 0.08

    rr2 =lax.broadcasted_iota(jnp.int32, (2 * LANES, 2 * LANES), 0)
    cc2 = lax.broadcasted_iota(jnp.int32, (2 * LANES, 2 * LANES), 1)
    ones_bd = ((rr2 // RW_HEAD) == (cc2 // RW_HEAD)).astype(BF16)
    rr = lax.broadcasted_iota(jnp.int32, (LANES, LANES), 0)
    cc = lax.broadcasted_iota(jnp.int32, (LANES, LANES), 1)

    kk = k * k_k
    kmod = k * (1.0 + (a - 1.0) * k_a)
    stats = _head_sum(jnp.concatenate([kk * kk, r * kmod * r_k], axis=0), ones_bd)
    kk = kk / jnp.maximum(jnp.sqrt(stats[:TB]), 1e-12)
    bonus_dot = stats[TB:]
    ahat = kk * a

    tr = lax.broadcasted_iota(jnp.int32, (TB, TB), 0)
    tc = lax.broadcasted_iota(jnp.int32, (TB, TB), 1)
    cum = _cumsum_rows(((tr >= tc) & ((tr // C) == (tc // C))).astype(BF16), lw)
    e_inc = jnp.exp(cum)
    e_inv = jnp.exp(-cum)
    r_dec = r * e_inc
    kk_dec = kk * jnp.exp(cum - lw)
    a_inv = ahat * e_inv
    k_inv = kmod * e_inv
    yield 0.15

    lane =lax.broadcasted_iota(jnp.int32, (C, LANES), 1)
    first = lane < RW_HEAD

    def stack(zp):
        return jnp.concatenate([jnp.where(first, zp, 0.0), jnp.where(first, 0.0, zp)], axis=0).astype(BF16)

    strict = rr > cc
    incl = rr >= cc
    eye = (rr == cc).astype(F32)
    n_sq = int(np.log2(C)) - 1
    n2 = 2 * C
    units = [(j, p) for j in range(TB // C) for p in range(P)]
    cut = lambda t, u: t[u[0] * C:(u[0] + 1) * C, u[1] * LANES:(u[1] + 1) * LANES]
    kks = {u: stack(cut(kk_dec, u)) for u in units}
    rs = {u: stack(cut(r_dec, u)) for u in units}
    vs = {u: stack(cut(v, u)) for u in units}
    ak = {u: jnp.concatenate([stack(cut(a_inv, u)), stack(cut(k_inv, u))], axis=0) for u in units}
    yield 0.2
    sc = {u: _dot_t(jnp.concatenate([kks[u], rs[u]], axis=0), ak[u]) for u in units}
    yield 0.27
    l_a = {u: jnp.where(strict, sc[u][:n2, :n2], 0.0) for u in units}
    lkv = {u: _dot(jnp.where(strict, sc[u][:n2, n2:], 0.0).astype(BF16), vs[u]) for u in units}
    a_cat = {u: jnp.where(jnp.concatenate([incl, incl], axis=1), sc[u][n2:], 0.0).astype(BF16) for u in units}
    yield 0.33
    t_inv = {u: eye - l_a[u] for u in units}
    pw = {u: _dot(l_a[u].astype(BF16), l_a[u].astype(BF16)).astype(BF16) for u in units}
    yield 0.38
    for lvl in range(n_sq - 1):
        both = {u: _dot(pw[u], jnp.concatenate([t_inv[u].astype(BF16), pw[u]], axis=1)) for u in units}
        t_inv = {u: t_inv[u] + both[u][:, :n2] for u in units}
        pw = {u: both[u][:, n2:].astype(BF16) for u in units}
        yield 0.38 + 0.06 * (lvl + 1)
    t_inv = {u: t_inv[u] + _dot(pw[u], t_inv[u].astype(BF16)) for u in units}
    yield 0.67
    rhs = {u: jnp.concatenate([-kks[u].astype(F32), -lkv[u]], axis=1).astype(BF16) for u in units}
    wv = {u: _dot(t_inv[u].astype(BF16), rhs[u]) for u in units}
    yield 0.73

    st = [s_ref[p] for p in range(P)]
    y_rows = []
    for j in range(TB // C):
        us = [(j, p) for p in range(P)]
        wr = [_dot_t(jnp.concatenate([wv[u][:, :LANES].astype(BF16), rs[u]], axis=0), st[u[1]].astype(BF16))
              for u in us]
        uv = [jnp.concatenate([(wr[p][:n2] + wv[u][:, LANES:]).astype(BF16), vs[u]], axis=0)
              for p, u in enumerate(us)]
        yst = [wr[p][n2:] + _dot(a_cat[u], uv[p]) for p, u in enumerate(us)]
        gamma_end = e_inc[(j + 1) * C - 1:(j + 1) * C]
        st = [(st[p] + _tdot(uv[p], ak[u])) * gamma_end[:, p * LANES:(p + 1) * LANES]
              for p, u in enumerate(us)]
        y_rows.append(jnp.concatenate([yst[p][:C] + yst[p][C:] for p in range(P)], axis=1))
        yield 0.73 + 0.2 * (j + 1) / (TB // C)
    for p in range(P):
        s_ref[p] = st[p]
    y = jnp.concatenate(y_rows, axis=0)

    inv_n = 1.0 / RW_HEAD
    yc = y - _head_sum(y, ones_bd, two_pass=True) * inv_n
    yn = yc * lax.rsqrt(_head_sum(yc * yc, ones_bd) * inv_n + RW_LN_EPS) * ln_w + ln_b
    y_ref[0] = (yn + bonus_dot * v) * g


def _rwkv_plan(z3, v_first, prm, *, rkv_col0, lora_col0, small_col0, width, rows):
    B, T, _ = z3.shape
    C = rows
    has_vres = v_first is not None
    lora_w = RW_DECAY_LORA + RW_AAA_LORA + RW_GATE_LORA
    tok = lambda wdt, col0, j=0: pl.BlockSpec((1, C, wdt), lambda b, t: (b, t, col0 // wdt + j))
    full = lambda arr: pl.BlockSpec(arr.shape, lambda b, t: (0,) * arr.ndim)
    tok_out = pl.BlockSpec((1, C, width), lambda b, t: (b, t, 0))

    args = [z3, z3, z3, z3]
    specs = [tok(width, rkv_col0, 0), tok(width, rkv_col0, 1), tok(width, rkv_col0, 2), tok(lora_w, lora_col0)]
    if has_vres:
        args += [z3, v_first]
        specs += [tok(LANES, small_col0), tok_out]
    weights = [prm["mu_rkv"], prm["mu_lora"], prm["vecs"], prm["w2"], prm["a2"], prm["g2"]]
    if has_vres:
        weights += [prm["mu_small"], prm["v2"]]
    args += weights
    specs += [full(wt) for wt in weights]

    y_shape = jax.ShapeDtypeStruct((B, T, width), F32)
    scratch = [pltpu.VMEM((width // LANES, LANES, LANES), F32)]
    scratch += [pltpu.VMEM((8, width), F32)] * 3 + [pltpu.VMEM((8, lora_w), F32)]
    if has_vres:
        scratch += [pltpu.VMEM((8, LANES), F32)]
    n_out = 1 if has_vres else 2
    return dict(
        body=functools.partial(_rwkv_kernel, has_vres=has_vres),
        args=args, in_specs=specs,
        out_shape=[y_shape] * n_out, out_specs=[tok_out] * n_out,
        scratch=scratch,
    )


def _pad_rows(w, row0, rows):
    return jnp.zeros((rows, w.shape[1]), w.dtype).at[row0:row0 + w.shape[0]].set(w)


def kernel(x, c, norm_w, final_norm_w, ada_w, ada_b, ffn_up, ffn_down, w_in, w_in_vres,
           ml_conv_w, ml_conv_b, ml_wq, ml_wk, ml_i_b, ml_f_b, ml_norm_w, ml_skip,
           rw_mu, rw_mu_vres, rw_w0, rw_w2, rw_a0, rw_a2, rw_v0, rw_v2, rw_g2,
           rw_k_k, rw_k_a, rw_r_k, rw_ln_w, rw_ln_b, hg_lb_logits, hg_norm_w,
           w_branch, w_out):
    B, T, D = x.shape
    depth = norm_w.shape[0]
    N = B * T
    W = D
    lora_w = RW_DECAY_LORA + RW_AAA_LORA + RW_GATE_LORA

    o_mlx, o_mlo, o_mli, o_mlf = 0, W, 2 * W, 2 * W + ML_HEADS
    o_rw = 2 * W + 2 * ML_HEADS
    o_hg = o_rw + 3 * W + lora_w
    o_gate = o_hg + 4 * W
    n_in = o_gate + N_BRANCH * D
    c_rkv, c_hg = 2 * W, 5 * W
    c_lora = c_hg + 4 * W
    c_small = c_lora + lora_w
    n_cols = c_small + LANES
    s_vlo = 2 * ML_HEADS

    lb_soft = jax.nn.softmax(hg_lb_logits.astype(F32), axis=0)
    lower_bounds = jnp.cumsum(lb_soft, axis=0) - lb_soft[0]

    mod_all = _ada_mod(c, ada_w, ada_b).reshape(depth, B, N_SUB * 3, D)
    nw3 = norm_w.reshape(depth * N_SUB, 1, D)
    ffn_up_b, ffn_down_b = ffn_up.astype(BF16), ffn_down.astype(BF16)
    w_branch_b, w_out_b = w_branch.astype(BF16), w_out.astype(BF16)
    x2 = x.reshape(N, D)
    v_first = None
    for l in range(depth):
        wl = w_in[l]
        small = jnp.concatenate([wl[:, o_mli:o_mli + 2 * ML_HEADS]]
                                + ([w_in_vres[l - 1]] if l > 0 else []), axis=1)
        small = jnp.pad(small, ((0, 0), (0, LANES - small.shape[1])))
        w_cat = jnp.concatenate([
            wl[:, o_mlx:o_mlx + 2 * W], wl[:, o_rw:o_rw + 3 * W], wl[:, o_hg:o_hg + 4 * W],
            wl[:, o_rw + 3 * W:o_rw + 3 * W + lora_w], small], axis=1).astype(BF16)
        w_cat = w_cat.reshape(D, n_cols // INP_TN, INP_TN).transpose(1, 0, 2)

        x2 = _ffn(x2, mod_all, nw3, ffn_up_b, ffn_down_b, final_norm_w,
                  layer=l, half=0, tokens_per_batch=T, final_norm=False)

        z = _inproj(x2, mod_all, nw3, w_cat, layer=l, tokens_per_batch=T)
        z3 = z.reshape(B, T, n_cols)

        rows = min(MIXER_ROWS, T)
        gates = z3[:, :, c_small:c_small + 2 * ML_HEADS].transpose(0, 2, 1)
        ml_plan = _mlstm_plan(z3, gates, jnp.concatenate([ml_i_b[l], ml_f_b[l]]), ml_conv_w[l], ml_conv_b[l],
                              ml_wq[l], ml_wk[l], ml_norm_w[l], ml_skip[l], width=W, rows=rows)

        mu = rw_mu[l]
        prm = {
            "mu_rkv": mu[:3 * W].reshape(3, W),
            "mu_lora": mu[3 * W:].reshape(1, lora_w),
            "vecs": jnp.stack([rw_w0[l], rw_a0[l], rw_k_k[l], rw_k_a[l], rw_ln_w[l], rw_ln_b[l],
                               rw_r_k[l].reshape(W), rw_v0[l - 1] if l > 0 else jnp.zeros((W,), F32)]),
            "w2": _pad_rows(rw_w2[l], 0, LANES).astype(BF16),
            "a2": _pad_rows(rw_a2[l], RW_DECAY_LORA, LANES).astype(BF16),
            "g2": rw_g2[l].astype(BF16),
        }
        if l > 0:
            prm["mu_small"] = jnp.zeros((1, LANES), F32).at[0, s_vlo:s_vlo + RW_MV_LORA].set(rw_mu_vres[l - 1])
            prm["v2"] = _pad_rows(rw_v2[l - 1], s_vlo, LANES).astype(BF16)
        rw_plan = _rwkv_plan(z3, v_first, prm, rkv_col0=c_rkv, lora_col0=c_lora,
                             small_col0=c_small, width=W, rows=rows)
        hg_plan = _hgrn_plan(z3, lower_bounds[l], hg_norm_w[l], col0=c_hg, width=W, rows=rows)
        outs = _run_plans([ml_plan, rw_plan, hg_plan], batch=B, steps=T // rows, name="mixers")
        y_ml, y_rw, y_hg = outs[0], outs[1], outs[-1]
        if l == 0:
            v_first = outs[2]

        x2 = _branch_mix(x2, mod_all, nw3, y_ml.reshape(N, W), y_rw.reshape(N, W), y_hg.reshape(N, W),
                         wl[:, o_gate:n_in].astype(BF16), w_branch_b, w_out_b, layer=l, tokens_per_batch=T)

        x2 = _ffn(x2, mod_all, nw3, ffn_up_b, ffn_down_b, final_norm_w,
                  layer=l, half=1, tokens_per_batch=T, final_norm=(l == depth - 1))
    return x2.reshape(B, T, D)
```

```python
import functools

import numpy as np
import jax
import jax.numpy as jnp
from jax import lax
from jax.experimental import pallas as pl
from jax.experimental.pallas import tpu as pltpu

F32 = jnp.float32
BF16 = jnp.bfloat16

EPS = 1e-6
NEG_BIG = -1e30
TINY = 1e-30

N_SUB = 3
N_BRANCH = 3
ML_HEADS = 4
ML_CONV = 4
RW_HEAD = 64
RW_LN_EPS = 64e-5
RW_DECAY_LORA = 64
RW_AAA_LORA = 64
RW_MV_LORA = 32
RW_GATE_LORA = 128
HG_EXPAND = 128

LANES = 128
VMEM_LIMIT = 56 * 1024 * 1024

FFN_TM = 1024
FFN_FC = 256
INP_TM = 1024
INP_TN = 1920
MIX_TM = 512
ML_CHUNK = 256
RW_CHUNK = 64
HG_CHUNK = 128
MIXER_ROWS = 256


def _cparams(sem):
    return pltpu.CompilerParams(dimension_semantics=sem, vmem_limit_bytes=VMEM_LIMIT)


def _dot(a, b):
    return jnp.dot(a, b, preferred_element_type=F32)


def _dot_t(a, b):
    return lax.dot_general(a, b, (((1,), (1,)), ((), ())), preferred_element_type=F32)


def _tdot(a, b):
    return lax.dot_general(a, b, (((0,), (0,)), ((), ())), preferred_element_type=F32)


def _split3(x):
    hi = x.astype(BF16)
    r1 = x - hi.astype(F32)
    mid = r1.astype(BF16)
    lo = (r1 - mid.astype(F32)).astype(BF16)
    return hi, mid, lo


def _cumsum_rows(tri, x):
    hi, mid, lo = _split3(x)
    return _dot(tri, hi) + (_dot(tri, mid) + _dot(tri, lo))


def _cumsum_lanes(x, tri):
    hi, mid, lo = _split3(x)
    return _dot(hi, tri) + (_dot(mid, tri) + _dot(lo, tri))


def _sigmoid(x):
    return 1.0 / (1.0 + jnp.exp(-x))


def _gate_sigmoid(x):
    return 0.5 * jnp.tanh(0.5 * x) + 0.5


def _silu(x):
    return x * _gate_sigmoid(x)


def _modulated_norm(x, nw, shift, scale):
    ms = jnp.mean(x * x, axis=-1, keepdims=True)
    return (x * lax.rsqrt(ms + EPS) * nw) * (1.0 + scale) + shift


def _ada_kernel(c_ref, w_ref, b_ref, o_ref):
    cond = _silu(c_ref[...])
    o_ref[0] = _dot(cond, w_ref[0]) + b_ref[0]


def _ada_mod(c, ada_w, ada_b):
    L, D, M = ada_w.shape
    B = c.shape[0]
    tn = M // 8
    return pl.pallas_call(
        _ada_kernel,
        out_shape=jax.ShapeDtypeStruct((L, B, M), F32),
        grid=(L, M // tn),
        in_specs=[
            pl.BlockSpec((B, D), lambda l, j: (0, 0)),
            pl.BlockSpec((1, D, tn), lambda l, j: (l, 0, j)),
            pl.BlockSpec((1, 1, tn), lambda l, j: (l, 0, j)),
        ],
        out_specs=pl.BlockSpec((1, B, tn), lambda l, j: (l, 0, j)),
        compiler_params=_cparams(("arbitrary", "arbitrary")),
        name="ada_mod",
    )(c, ada_w, ada_b.reshape(L, 1, M))


def _ffn_kernel(x_ref, mod_ref, nw_ref, wu_ref, wd_ref, fw_ref, o_ref, acc_ref, *, sub, final_norm):
    x = x_ref[...]
    mod = mod_ref[0, 0]
    shift, scale, gate = (mod[3 * sub + i:3 * sub + i + 1] for i in range(3))
    hb = _modulated_norm(x, nw_ref[0], shift, scale).astype(BF16)
    F = wd_ref.shape[2]
    for c in range(F // FFN_FC):
        cols = slice(c * FFN_FC, (c + 1) * FFN_FC)
        a = _dot(hb, wu_ref[0, 0, :, cols])
        b = _dot(hb, wu_ref[0, 0, :, F + c * FFN_FC:F + (c + 1) * FFN_FC])
        act = (_silu(a) * b).astype(BF16)
        d = _dot(act, wd_ref[0, 0, cols, :])
        if c == 0:
            acc_ref[...] = d
        else:
            acc_ref[...] += d
    y = x + (0.5 * (1.0 + gate)) * acc_ref[...]
    if final_norm:
        ms = jnp.mean(y * y, axis=-1, keepdims=True)
        y = y * lax.rsqrt(ms + EPS) * fw_ref[...]
    o_ref[...] = y


def _ffn(x, mod_all, norm_w, w_up, w_down, fw, *, layer, half, tokens_per_batch, final_norm):
    N, D = x.shape
    sub = 2 * half
    tm = min(FFN_TM, tokens_per_batch)
    bpb = tokens_per_batch // tm
    return pl.pallas_call(
        functools.partial(_ffn_kernel, sub=sub, final_norm=final_norm),
        out_shape=jax.ShapeDtypeStruct((N, D), F32),
        grid=(N // tm,),
        in_specs=[
            pl.BlockSpec((tm, D), lambda i: (i, 0)),
            pl.BlockSpec((1, 1) + mod_all.shape[2:], lambda i: (layer, i // bpb, 0, 0)),
            pl.BlockSpec((1, 1, D), lambda i: (N_SUB * layer + sub, 0, 0)),
            pl.BlockSpec((1, 1) + w_up.shape[2:], lambda i: (layer, half, 0, 0), pipeline_mode=pl.Buffered(1)),
            pl.BlockSpec((1, 1) + w_down.shape[2:], lambda i: (layer, half, 0, 0), pipeline_mode=pl.Buffered(1)),
            pl.BlockSpec((1, D), lambda i: (0, 0)),
        ],
        out_specs=pl.BlockSpec((tm, D), lambda i: (i, 0)),
        scratch_shapes=[pltpu.VMEM((tm, D), F32)],
        compiler_params=_cparams(("arbitrary",)),
        name="ffn_half",
    )(x, mod_all, norm_w, w_up, w_down, fw.reshape(1, D))


def _inproj_kernel(x_ref, mod_ref, nw_ref, w_ref, o_ref, h_ref, *, sub):
    @pl.when(pl.program_id(1) == 0)
    def _():
        mod = mod_ref[0, 0]
        shift, scale = mod[3 * sub:3 * sub + 1], mod[3 * sub + 1:3 * sub + 2]
        h_ref[...] = _modulated_norm(x_ref[...], nw_ref[0], shift, scale).astype(BF16)

    o_ref[...] = _dot(h_ref[...], w_ref[pl.program_id(1)])


def _inproj(x, mod_all, norm_w, w, *, layer, tokens_per_batch):
    N, D = x.shape
    n_tiles = w.shape[0]
    sub = 1
    tm = min(INP_TM, tokens_per_batch)
    bpb = tokens_per_batch // tm
    return pl.pallas_call(
        functools.partial(_inproj_kernel, sub=sub),
        out_shape=jax.ShapeDtypeStruct((N, n_tiles * INP_TN), F32),
        grid=(N // tm, n_tiles),
        in_specs=[
            pl.BlockSpec((tm, D), lambda i, j: (i, 0)),
            pl.BlockSpec((1, 1) + mod_all.shape[2:], lambda i, j: (layer, i // bpb, 0, 0)),
            pl.BlockSpec((1, 1, D), lambda i, j: (N_SUB * layer + sub, 0, 0)),
            pl.BlockSpec(w.shape, lambda i, j: (0, 0, 0), pipeline_mode=pl.Buffered(1)),
        ],
        out_specs=pl.BlockSpec((tm, INP_TN), lambda i, j: (i, j)),
        scratch_shapes=[pltpu.VMEM((tm, D), BF16)],
        compiler_params=_cparams(("arbitrary", "arbitrary")),
        name="mixer_inproj",
    )(x, mod_all, norm_w, w)


def _mix_kernel(x_ref, mod_ref, nw_ref, yml_ref, yrw_ref, yhg_ref, wg_ref, wbr_ref, wout_ref, o_ref,
                *, sub):
    x = x_ref[...]
    D = x.shape[1]
    mod = mod_ref[0, 0]
    shift, scale, gate = (mod[3 * sub + i:3 * sub + i + 1] for i in range(3))
    hb = _modulated_norm(x, nw_ref[0], shift, scale).astype(BF16)
    mixed = None
    for n, y_ref in enumerate((yml_ref, yrw_ref, yhg_ref)):
        gate_pre = _dot(hb, wg_ref[:, n * D:(n + 1) * D])
        term = _gate_sigmoid(gate_pre) * _dot(y_ref[...].astype(BF16), wbr_ref[0, n])
        mixed = term if mixed is None else mixed + term
    o_ref[...] = x + (1.0 + gate) * _dot(mixed.astype(BF16), wout_ref[0])


def _branch_mix(x, mod_all, norm_w, y_ml, y_rw, y_hg, w_gate, w_branch, w_out, *, layer, tokens_per_batch):
    N, D = x.shape
    sub = 1
    bpb = tokens_per_batch // MIX_TM
    tok = pl.BlockSpec((MIX_TM, D), lambda i: (i, 0))
    return pl.pallas_call(
        functools.partial(_mix_kernel, sub=sub),
        out_shape=jax.ShapeDtypeStruct((N, D), F32),
        grid=(N // MIX_TM,),
        in_specs=[
            tok,
            pl.BlockSpec((1, 1) + mod_all.shape[2:], lambda i: (layer, i // bpb, 0, 0)),
            pl.BlockSpec((1, 1, D), lambda i: (N_SUB * layer + sub, 0, 0)),
            tok, tok, tok,
            pl.BlockSpec(w_gate.shape, lambda i: (0, 0), pipeline_mode=pl.Buffered(1)),
            pl.BlockSpec((1,) + w_branch.shape[1:], lambda i: (layer, 0, 0, 0), pipeline_mode=pl.Buffered(1)),
            pl.BlockSpec((1,) + w_out.shape[1:], lambda i: (layer, 0, 0), pipeline_mode=pl.Buffered(1)),
        ],
        out_specs=tok,
        compiler_params=_cparams(("arbitrary",)),
        name="branch_mix",
    )(x, mod_all, norm_w, y_ml, y_rw, y_hg, w_gate, w_branch, w_out)


def _log_sigmoid(x):
    return jnp.minimum(x, 0.0) - jnp.log1p(jnp.exp(-jnp.abs(x)))


def _mlstm_kernel(bias_ref, x_ref, op_ref, g_ref, cw_ref, cb_ref, wq_ref, wk_ref,
                  nw_ref, sk_ref, y_ref, c_ref, m_ref, tail_ref):
    TB, W = x_ref.shape[1], x_ref.shape[2]
    L = min(ML_CHUNK, TB)
    H = ML_HEADS
    dh = W // H
    heads = range(H)
    chunks = range(TB // L)
    sls = [slice(h * dh, (h + 1) * dh) for h in heads]
    rws = [slice(j * L, (j + 1) * L) for j in chunks]

    x = x_ref[0]
    cw = cw_ref[...]
    tail = tail_ref[...]
    row8 = lax.broadcasted_iota(jnp.int32, (8, W), 0)
    conv = x * cw[ML_CONV - 1:ML_CONV]
    for s in range(1, ML_CONV):
        xr = pltpu.roll(x, s, 0)
        top = jnp.where(row8 < s, pltpu.roll(tail, s, 0), xr[0:8])
        xs = jnp.concatenate([top, xr[8:]], axis=0)
        conv = conv + xs * cw[ML_CONV - 1 - s:ML_CONV - s]
    tail_ref[...] = x[TB - 8:TB]
    xc = _silu(conv + cb_ref[...])
    xcb = xc.astype(BF16)
    vb = x.astype(BF16)

    g = g_ref[0]
    li_all = jnp.concatenate([g[h:h + 1] + bias_ref[h] for h in heads], axis=0)
    lf_all = _log_sigmoid(jnp.concatenate([g[H + h:H + h + 1] + bias_ref[H + h] for h in heads], axis=0))
    r = lax.broadcasted_iota(jnp.int32, (L, L), 0)
    c = lax.broadcasted_iota(jnp.int32, (L, L), 1)
    causal = r >= c
    tri = (r <= c).astype(BF16)

    lane = lax.broadcasted_iota(jnp.int32, (H, L), 1)
    m_prev = m_ref[...][:, 0:1]
    g_rows, decay, col_rows = [], [], []
    for rw in rws:
        li = li_all[:, rw]
        b_rows = _cumsum_lanes(lf_all[:, rw], tri)
        g_rows.append(li - b_rows)
        cm = g_rows[-1]
        sh = 1
        while sh < L:
            cm = jnp.maximum(cm, jnp.where(lane >= sh, pltpu.roll(cm, sh, 1), NEG_BIG))
            sh *= 2
        mx = jnp.maximum(m_prev, cm)
        b_end = b_rows[:, L - 1:L]
        log_ws = b_end - b_rows + li
        m_new = jnp.maximum(b_end + m_prev, jnp.max(log_ws, axis=-1, keepdims=True))
        decay.append(jnp.exp(b_end + m_prev - m_new))
        col_rows += [-mx, jnp.exp(m_prev - mx), jnp.exp(-(b_rows + mx)), jnp.exp(log_ws - m_new)]
        m_prev = m_new
    m_ref[...] = jnp.broadcast_to(m_prev, (H, LANES))
    cols = jnp.concatenate(col_rows + [jnp.zeros((LANES - 4 * H * len(rws), L), F32)], axis=0).T

    ones_col = (lax.broadcasted_iota(jnp.int32, (TB, LANES), 1) == 0).astype(BF16)
    vaug = [jnp.concatenate([vb[:, sl], ones_col], axis=1) for sl in sls]

    q = [_dot(xcb[:, sl], wq_ref[h]) for h, sl in zip(heads, sls)]
    k = [_dot(xcb[:, sl], wk_ref[h]) * (dh ** -0.5) for h, sl in zip(heads, sls)]
    qb = [t.astype(BF16) for t in q]
    kb = [t.astype(BF16) for t in k]
    units = [(j, h) for j in chunks for h in heads]
    col = lambda u, i: cols[:, (4 * u[0] + i) * H + u[1]:(4 * u[0] + i) * H + u[1] + 1]
    qk = {u: _dot_t(qb[u[1]][rws[u[0]]], kb[u[1]][rws[u[0]]]) for u in units}
    sv, kw = {}, {}
    for u in units:
        s_mat = qk[u] * jnp.where(causal, jnp.exp(col(u, 0) + g_rows[u[0]][u[1]:u[1] + 1]), 0.0)
        sv[u] = _dot(s_mat.astype(BF16), vaug[u[1]][rws[u[0]]])
        kw[u] = (k[u[1]][rws[u[0]]] * col(u, 3)).astype(BF16)

    state = [c_ref[h] for h in heads]
    hs = {}
    for j in chunks:
        qc = [_dot(qb[h][rws[j]], state[h].astype(BF16)) for h in heads]
        for h in heads:
            u = (j, h)
            both = col(u, 1) * qc[h] + sv[u]
            inv = 1.0 / jnp.maximum(jnp.abs(both[:, dh:dh + 1]), col(u, 2))
            hs[u] = both[:, :dh] * inv
        state = [decay[j][h:h + 1] * state[h] + _tdot(kw[(j, h)], vaug[h][rws[j]]) for h in heads]
    for h in heads:
        c_ref[h] = state[h]

    gate = _gate_sigmoid(op_ref[0])
    nw, sk = nw_ref[...], sk_ref[...]
    for h, sl in zip(heads, sls):
        hg = jnp.concatenate([hs[(j, h)] for j in chunks], axis=0) * gate[:, sl]
        hg = hg - jnp.mean(hg, axis=-1, keepdims=True)
        hn = hg * lax.rsqrt(jnp.mean(hg * hg, axis=-1, keepdims=True) + EPS)
        y_ref[0, :, sl] = hn * nw[:, sl] + sk[:, sl] * xc[:, sl]


def _mlstm_plan(z3, gates, bias, conv_w, conv_b, wq, wk, norm_w, skip, *, width, rows):
    B, T, _ = z3.shape
    dh = width // ML_HEADS
    vec = lambda b, t: (0, 0)
    return dict(
        body=_mlstm_kernel,
        args=[bias, z3, z3, gates, conv_w, conv_b.reshape(1, width), wq.astype(BF16),
              wk.astype(BF16), norm_w.reshape(1, width), skip.reshape(1, width)],
        in_specs=[
            pl.BlockSpec(memory_space=pltpu.SMEM),
            pl.BlockSpec((1, rows, width), lambda b, t: (b, t, 0)),
            pl.BlockSpec((1, rows, width), lambda b, t: (b, t, 1)),
            pl.BlockSpec((1, 2 * ML_HEADS, rows), lambda b, t: (b, 0, t)),
            pl.BlockSpec((ML_CONV, width), vec),
            pl.BlockSpec((1, width), vec),
            pl.BlockSpec((ML_HEADS, dh, dh), lambda b, t: (0, 0, 0)),
            pl.BlockSpec((ML_HEADS, dh, dh), lambda b, t: (0, 0, 0)),
            pl.BlockSpec((1, width), vec),
            pl.BlockSpec((1, width), vec),
        ],
        out_shape=[jax.ShapeDtypeStruct((B, T, width), F32)],
        out_specs=[pl.BlockSpec((1, rows, width), lambda b, t: (b, t, 0))],
        scratch=[
            pltpu.VMEM((ML_HEADS, dh, dh + LANES), F32),
            pltpu.VMEM((ML_HEADS, LANES), F32),
            pltpu.VMEM((8, width), F32),
        ],
    )


def _zero_refs(refs):
    for ref in refs:
        ref[...] = jnp.zeros_like(ref)


def _plans_kernel(*refs, plans):
    n_in = [len(p["args"]) for p in plans]
    n_out = [len(p["out_shape"]) for p in plans]
    n_scr = [len(p["scratch"]) for p in plans]
    ins, outs, scr = refs[:sum(n_in)], refs[sum(n_in):sum(n_in) + sum(n_out)], refs[sum(n_in) + sum(n_out):]
    pl.when(pl.program_id(1) == 0)(functools.partial(_zero_refs, scr))
    i = o = s = 0
    for p, ni, no, ns in zip(plans, n_in, n_out, n_scr):
        p["body"](*ins[i:i + ni], *outs[o:o + no], *scr[s:s + ns])
        i, o, s = i + ni, o + no, s + ns


def _run_plans(plans, *, batch, steps, name):
    static = [dict(body=p["body"], args=[None] * len(p["args"]), out_shape=[None] * len(p["out_shape"]),
                   scratch=[None] * len(p["scratch"])) for p in plans]
    return pl.pallas_call(
        functools.partial(_plans_kernel, plans=static),
        out_shape=[s for p in plans for s in p["out_shape"]],
        grid=(batch, steps),
        in_specs=[s for p in plans for s in p["in_specs"]],
        out_specs=[s for p in plans for s in p["out_specs"]],
        scratch_shapes=[s for p in plans for s in p["scratch"]],
        compiler_params=_cparams(("arbitrary", "arbitrary")),
        name=name,
    )(*[a for p in plans for a in p["args"]])


def _midrow_broadcast(b, m):
    n, w = b.shape
    row = lambda i: b[i:i + 1]
    if m >= 8:
        return jnp.concatenate(
            [jnp.broadcast_to(row(p * 2 * m + m), (2 * m, w)) for p in range(n // (2 * m))], axis=0)
    pick = lambda off: jnp.concatenate(
        [jnp.broadcast_to(row(8 * g + off), (8, w)) for g in range(n // 8)], axis=0)
    sub = lax.broadcasted_iota(jnp.int32, (n, w), 0) % 8
    if m == 4:
        return pick(4)
    if m == 2:
        return jnp.where(sub < 4, pick(2), pick(6))
    return jnp.where(sub < 2, pick(1), jnp.where(sub < 4, pick(3), jnp.where(sub < 6, pick(5), pick(7))))


def _hgrn_kernel(q_ref, f_ref, i_ref, g_ref, lb_ref, nw_ref, y_ref, s_ref):
    TB, W = q_ref.shape[1], q_ref.shape[2]
    C = min(HG_CHUNK, TB)
    H = W // HG_EXPAND
    chunks = range(TB // C)
    rws = [slice(j * C, (j + 1) * C) for j in chunks]

    lb = lb_ref[...]
    sig = _sigmoid(f_ref[0])
    q = _silu(q_ref[0])
    k = (1.0 - lb) * (1.0 - sig)
    log_f = jnp.log(jnp.maximum(lb + (1.0 - lb) * sig, TINY))
    rt = lax.broadcasted_iota(jnp.int32, (TB, TB), 0)
    ct = lax.broadcasted_iota(jnp.int32, (TB, TB), 1)
    b = _cumsum_rows(((rt >= ct) & ((rt // C) == (ct // C))).astype(BF16), log_f)
    v = i_ref[0]
    vb = v.astype(BF16)

    r = lax.broadcasted_iota(jnp.int32, (C, C), 0)
    c = lax.broadcasted_iota(jnp.int32, (C, C), 1)
    rowi = lax.broadcasted_iota(jnp.int32, (TB, W), 0)
    qb, kb = q.astype(BF16), k.astype(BF16)
    heads = range(H)
    sls = [slice(h * HG_EXPAND, (h + 1) * HG_EXPAND) for h in heads]
    units = [(j, h) for j in chunks for h in heads]
    a = {u: jnp.zeros((C, C), F32) for u in units}
    m = 1
    while m < C:
        e = jnp.exp(-jnp.abs(b - _midrow_broadcast(b, m)))
        z_l = jnp.where((rowi % (2 * m)) >= m, qb, kb) * e.astype(BF16)
        mask = ((r // (2 * m)) == (c // (2 * m))) & ((r % (2 * m)) >= m) & ((c % (2 * m)) < m)
        for j, h in units:
            zu = z_l[rws[j], sls[h]]
            a[(j, h)] = jnp.where(mask, _dot_t(zu, zu), a[(j, h)])
        m *= 2

    qe = (q * jnp.exp(b)).astype(BF16)
    b_ends = [b[(j + 1) * C - 1:(j + 1) * C] for j in chunks]
    b_end_rows = jnp.concatenate([jnp.broadcast_to(b_ends[j], (C, W)) for j in chunks], axis=0)
    kd = (k * jnp.exp(b_end_rows - b)).astype(BF16)
    diag = q * k
    intra = {(j, h): _dot(a[(j, h)].astype(BF16), vb[rws[j], sls[h]])
             + jnp.sum(diag[rws[j], sls[h]], axis=-1, keepdims=True) * v[rws[j], sls[h]] for j, h in units}
    kv = {(j, h): _tdot(vb[rws[j], sls[h]], kd[rws[j], sls[h]]) for j, h in units}
    st = [s_ref[h] for h in heads]
    o_rows = []
    for j in chunks:
        inter = [_dot_t(qe[rws[j], sls[h]], st[h].astype(BF16)) for h in heads]
        g_end = jnp.exp(b_ends[j])
        st = [st[h] * g_end[:, sls[h]] + kv[(j, h)] for h in heads]
        o_rows.append([inter[h] + intra[(j, h)] for h in heads])
    for h in heads:
        s_ref[h] = st[h]
    gate = _gate_sigmoid(g_ref[0])
    nw = nw_ref[...]
    for h, sl in zip(heads, sls):
        o = jnp.concatenate([o_rows[j][h] for j in chunks], axis=0)
        on = o * lax.rsqrt(jnp.mean(o * o, axis=-1, keepdims=True) + EPS)
        y_ref[0, :, sl] = on * nw[:, sl] * gate[:, sl]


def _hgrn_plan(z3, lower_bound, norm_w, *, col0, width, rows):
    B, T, _ = z3.shape
    H = width // HG_EXPAND
    blk0 = col0 // width
    spec = lambda j: pl.BlockSpec((1, rows, width), lambda b, t: (b, t, blk0 + j))
    vec = pl.BlockSpec((1, width), lambda b, t: (0, 0))
    return dict(
        body=_hgrn_kernel,
        args=[z3, z3, z3, z3, lower_bound.reshape(1, width), norm_w.reshape(1, width)],
        in_specs=[spec(0), spec(1), spec(2), spec(3), vec, vec],
        out_shape=[jax.ShapeDtypeStruct((B, T, width), F32)],
        out_specs=[pl.BlockSpec((1, rows, width), lambda b, t: (b, t, 0))],
        scratch=[pltpu.VMEM((H, HG_EXPAND, HG_EXPAND), F32)],
    )


def _token_shift(z, mu, last_ref):
    n = z.shape[0]
    row = lax.broadcasted_iota(jnp.int32, z.shape, 0)
    prev = jnp.where(row == 0, last_ref[7:8], pltpu.roll(z, 1, 0))
    last_ref[...] = z[n - 8:n]
    return z + mu * (prev - z)


def _head_sum(x, ones_bd, two_pass=False):
    n, w = x.shape
    slab = ones_bd.shape[0]
    xs = jnp.concatenate([x[:, g * slab:(g + 1) * slab] for g in range(w // slab)], axis=0)
    hi = xs.astype(BF16)
    s = _dot(hi, ones_bd)
    if two_pass:
        s = s + _dot((xs - hi.astype(F32)).astype(BF16), ones_bd)
    return jnp.concatenate([s[g * n:(g + 1) * n] for g in range(w // slab)], axis=1)


def _softplus(x):
    return jnp.maximum(x, 0.0) + jnp.log1p(jnp.exp(-jnp.abs(x)))


def _rwkv_kernel(*refs, has_vres):
    it = iter(refs)
    r_ref, k_ref, v_ref, lo_ref = next(it), next(it), next(it), next(it)
    sm_ref, vf_ref = (next(it), next(it)) if has_vres else (None, None)
    mu_ref, mulo_ref, vec_ref, w2_ref, a2_ref, g2_ref = (next(it) for _ in range(6))
    musm_ref, v2_ref = (next(it), next(it)) if has_vres else (None, None)
    y_ref = next(it)
    vfo_ref = None if has_vres else next(it)
    s_ref, pr_ref, pk_ref, pv_ref, plo_ref = (next(it) for _ in range(5))
    psm_ref = next(it) if has_vres else None

    TB, W = r_ref.shape[1], r_ref.shape[2]
    C = min(RW_CHUNK, TB)
    P = W // LANES

    vec = vec_ref[...]
    w0, a0, k_k, k_a, ln_w, ln_b, r_k, v0 = (vec[i:i + 1] for i in range(8))
    mu = mu_ref[...]
    r = _token_shift(r_ref[0], mu[0:1], pr_ref)
    k = _token_shift(k_ref[0], mu[1:2], pk_ref)
    v = _token_shift(v_ref[0], mu[2:3], pv_ref)
    lo = _token_shift(lo_ref[0], mulo_ref[...], plo_ref)
    lo_wa, lo_g = lo[:, :LANES], lo[:, LANES:]

    w_raw = -_softplus(-(w0 + _dot(jnp.tanh(lo_wa).astype(BF16), w2_ref[...]))) - 0.5
    lw = -jnp.exp(w_raw)
    a = _gate_sigmoid(a0 + _dot(lo_wa.astype(BF16), a2_ref[...]))
    g = _dot(_gate_sigmoid(lo_g).astype(BF16), g2_ref[...])
    if has_vres:
        sm = _token_shift(sm_ref[0], musm_ref[...], psm_ref)
        v = v + (vf_ref[0] - v) * _gate_sigmoid(v0 + _dot(sm.astype(BF16), v2_ref[...]))
    else:
        vfo_ref[0] = v

    rr2 =lax.broadcasted_iota(jnp.int32, (2 * LANES, 2 * LANES), 0)
    cc2 = lax.broadcasted_iota(jnp.int32, (2 * LANES, 2 * LANES), 1)
    ones_bd = ((rr2 // RW_HEAD) == (cc2 // RW_HEAD)).astype(BF16)
    rr = lax.broadcasted_iota(jnp.int32, (LANES, LANES), 0)
    cc = lax.broadcasted_iota(jnp.int32, (LANES, LANES), 1)

    kk = k * k_k
    kmod = k * (1.0 + (a - 1.0) * k_a)
    stats = _head_sum(jnp.concatenate([kk * kk, r * kmod * r_k], axis=0), ones_bd)
    kk = kk / jnp.maximum(jnp.sqrt(stats[:TB]), 1e-12)
    bonus_dot = stats[TB:]
    ahat = kk * a

    tr = lax.broadcasted_iota(jnp.int32, (TB, TB), 0)
    tc = lax.broadcasted_iota(jnp.int32, (TB, TB), 1)
    cum = _cumsum_rows(((tr >= tc) & ((tr // C) == (tc // C))).astype(BF16), lw)
    e_inc = jnp.exp(cum)
    e_inv = jnp.exp(-cum)
    r_dec = r * e_inc
    kk_dec = kk * jnp.exp(cum - lw)
    a_inv = ahat * e_inv
    k_inv = kmod * e_inv

    lane =lax.broadcasted_iota(jnp.int32, (C, LANES), 1)
    first = lane < RW_HEAD

    def stack(zp):
        return jnp.concatenate([jnp.where(first, zp, 0.0), jnp.where(first, 0.0, zp)], axis=0).astype(BF16)

    strict = rr > cc
    incl = rr >= cc
    eye = (rr == cc).astype(F32)
    n_sq = int(np.log2(C)) - 1
    n2 = 2 * C
    units = [(j, p) for j in range(TB // C) for p in range(P)]
    cut = lambda t, u: t[u[0] * C:(u[0] + 1) * C, u[1] * LANES:(u[1] + 1) * LANES]
    kks = {u: stack(cut(kk_dec, u)) for u in units}
    rs = {u: stack(cut(r_dec, u)) for u in units}
    vs = {u: stack(cut(v, u)) for u in units}
    ak = {u: jnp.concatenate([stack(cut(a_inv, u)), stack(cut(k_inv, u))], axis=0) for u in units}
    sc = {u: _dot_t(jnp.concatenate([kks[u], rs[u]], axis=0), ak[u]) for u in units}
    l_a = {u: jnp.where(strict, sc[u][:n2, :n2], 0.0) for u in units}
    lkv = {u: _dot(jnp.where(strict, sc[u][:n2, n2:], 0.0).astype(BF16), vs[u]) for u in units}
    a_cat = {u: jnp.where(jnp.concatenate([incl, incl], axis=1), sc[u][n2:], 0.0).astype(BF16) for u in units}
    t_inv = {u: eye - l_a[u] for u in units}
    pw = {u: _dot(l_a[u].astype(BF16), l_a[u].astype(BF16)).astype(BF16) for u in units}
    for _ in range(n_sq - 1):
        both = {u: _dot(pw[u], jnp.concatenate([t_inv[u].astype(BF16), pw[u]], axis=1)) for u in units}
        t_inv = {u: t_inv[u] + both[u][:, :n2] for u in units}
        pw = {u: both[u][:, n2:].astype(BF16) for u in units}
    t_inv = {u: t_inv[u] + _dot(pw[u], t_inv[u].astype(BF16)) for u in units}
    rhs = {u: jnp.concatenate([-kks[u].astype(F32), -lkv[u]], axis=1).astype(BF16) for u in units}
    wv = {u: _dot(t_inv[u].astype(BF16), rhs[u]) for u in units}

    st = [s_ref[p] for p in range(P)]
    y_rows = []
    for j in range(TB // C):
        us = [(j, p) for p in range(P)]
        wr = [_dot_t(jnp.concatenate([wv[u][:, :LANES].astype(BF16), rs[u]], axis=0), st[u[1]].astype(BF16))
              for u in us]
        uv = [jnp.concatenate([(wr[p][:n2] + wv[u][:, LANES:]).astype(BF16), vs[u]], axis=0)
              for p, u in enumerate(us)]
        yst = [wr[p][n2:] + _dot(a_cat[u], uv[p]) for p, u in enumerate(us)]
        gamma_end = e_inc[(j + 1) * C - 1:(j + 1) * C]
        st = [(st[p] + _tdot(uv[p], ak[u])) * gamma_end[:, p * LANES:(p + 1) * LANES]
              for p, u in enumerate(us)]
        y_rows.append(jnp.concatenate([yst[p][:C] + yst[p][C:] for p in range(P)], axis=1))
    for p in range(P):
        s_ref[p] = st[p]
    y = jnp.concatenate(y_rows, axis=0)

    inv_n = 1.0 / RW_HEAD
    yc = y - _head_sum(y, ones_bd, two_pass=True) * inv_n
    yn = yc * lax.rsqrt(_head_sum(yc * yc, ones_bd) * inv_n + RW_LN_EPS) * ln_w + ln_b
    y_ref[0] = (yn + bonus_dot * v) * g


def _rwkv_plan(z3, v_first, prm, *, rkv_col0, lora_col0, small_col0, width, rows):
    B, T, _ = z3.shape
    C = rows
    has_vres = v_first is not None
    lora_w = RW_DECAY_LORA + RW_AAA_LORA + RW_GATE_LORA
    tok = lambda wdt, col0, j=0: pl.BlockSpec((1, C, wdt), lambda b, t: (b, t, col0 // wdt + j))
    full = lambda arr: pl.BlockSpec(arr.shape, lambda b, t: (0,) * arr.ndim)
    tok_out = pl.BlockSpec((1, C, width), lambda b, t: (b, t, 0))

    args = [z3, z3, z3, z3]
    specs = [tok(width, rkv_col0, 0), tok(width, rkv_col0, 1), tok(width, rkv_col0, 2), tok(lora_w, lora_col0)]
    if has_vres:
        args += [z3, v_first]
        specs += [tok(LANES, small_col0), tok_out]
    weights = [prm["mu_rkv"], prm["mu_lora"], prm["vecs"], prm["w2"], prm["a2"], prm["g2"]]
    if has_vres:
        weights += [prm["mu_small"], prm["v2"]]
    args += weights
    specs += [full(wt) for wt in weights]

    y_shape = jax.ShapeDtypeStruct((B, T, width), F32)
    scratch = [pltpu.VMEM((width // LANES, LANES, LANES), F32)]
    scratch += [pltpu.VMEM((8, width), F32)] * 3 + [pltpu.VMEM((8, lora_w), F32)]
    if has_vres:
        scratch += [pltpu.VMEM((8, LANES), F32)]
    n_out = 1 if has_vres else 2
    return dict(
        body=functools.partial(_rwkv_kernel, has_vres=has_vres),
        args=args, in_specs=specs,
        out_shape=[y_shape] * n_out, out_specs=[tok_out] * n_out,
        scratch=scratch,
    )


def _pad_rows(w, row0, rows):
    return jnp.zeros((rows, w.shape[1]), w.dtype).at[row0:row0 + w.shape[0]].set(w)


def kernel(x, c, norm_w, final_norm_w, ada_w, ada_b, ffn_up, ffn_down, w_in, w_in_vres,
           ml_conv_w, ml_conv_b, ml_wq, ml_wk, ml_i_b, ml_f_b, ml_norm_w, ml_skip,
           rw_mu, rw_mu_vres, rw_w0, rw_w2, rw_a0, rw_a2, rw_v0, rw_v2, rw_g2,
           rw_k_k, rw_k_a, rw_r_k, rw_ln_w, rw_ln_b, hg_lb_logits, hg_norm_w,
           w_branch, w_out):
    B, T, D = x.shape
    depth = norm_w.shape[0]
    N = B * T
    W = D
    lora_w = RW_DECAY_LORA + RW_AAA_LORA + RW_GATE_LORA

    o_mlx, o_mlo, o_mli, o_mlf = 0, W, 2 * W, 2 * W + ML_HEADS
    o_rw = 2 * W + 2 * ML_HEADS
    o_hg = o_rw + 3 * W + lora_w
    o_gate = o_hg + 4 * W
    n_in = o_gate + N_BRANCH * D
    c_rkv, c_hg = 2 * W, 5 * W
    c_lora = c_hg + 4 * W
    c_small = c_lora + lora_w
    n_cols = c_small + LANES
    s_vlo = 2 * ML_HEADS

    lb_soft = jax.nn.softmax(hg_lb_logits.astype(F32), axis=0)
    lower_bounds = jnp.cumsum(lb_soft, axis=0) - lb_soft[0]

    mod_all = _ada_mod(c, ada_w, ada_b).reshape(depth, B, N_SUB * 3, D)
    nw3 = norm_w.reshape(depth * N_SUB, 1, D)
    ffn_up_b, ffn_down_b = ffn_up.astype(BF16), ffn_down.astype(BF16)
    w_branch_b, w_out_b = w_branch.astype(BF16), w_out.astype(BF16)
    x2 = x.reshape(N, D)
    v_first = None
    for l in range(depth):
        wl = w_in[l]
        small = jnp.concatenate([wl[:, o_mli:o_mli + 2 * ML_HEADS]]
                                + ([w_in_vres[l - 1]] if l > 0 else []), axis=1)
        small = jnp.pad(small, ((0, 0), (0, LANES - small.shape[1])))
        w_cat = jnp.concatenate([
            wl[:, o_mlx:o_mlx + 2 * W], wl[:, o_rw:o_rw + 3 * W], wl[:, o_hg:o_hg + 4 * W],
            wl[:, o_rw + 3 * W:o_rw + 3 * W + lora_w], small], axis=1).astype(BF16)
        w_cat = w_cat.reshape(D, n_cols // INP_TN, INP_TN).transpose(1, 0, 2)

        x2 = _ffn(x2, mod_all, nw3, ffn_up_b, ffn_down_b, final_norm_w,
                  layer=l, half=0, tokens_per_batch=T, final_norm=False)

        z = _inproj(x2, mod_all, nw3, w_cat, layer=l, tokens_per_batch=T)
        z3 = z.reshape(B, T, n_cols)

        rows = min(MIXER_ROWS, T)
        gates = z3[:, :, c_small:c_small + 2 * ML_HEADS].transpose(0, 2, 1)
        ml_plan = _mlstm_plan(z3, gates, jnp.concatenate([ml_i_b[l], ml_f_b[l]]), ml_conv_w[l], ml_conv_b[l],
                              ml_wq[l], ml_wk[l], ml_norm_w[l], ml_skip[l], width=W, rows=rows)

        mu = rw_mu[l]
        prm = {
            "mu_rkv": mu[:3 * W].reshape(3, W),
            "mu_lora": mu[3 * W:].reshape(1, lora_w),
            "vecs": jnp.stack([rw_w0[l], rw_a0[l], rw_k_k[l], rw_k_a[l], rw_ln_w[l], rw_ln_b[l],
                               rw_r_k[l].reshape(W), rw_v0[l - 1] if l > 0 else jnp.zeros((W,), F32)]),
            "w2": _pad_rows(rw_w2[l], 0, LANES).astype(BF16),
            "a2": _pad_rows(rw_a2[l], RW_DECAY_LORA, LANES).astype(BF16),
            "g2": rw_g2[l].astype(BF16),
        }
        if l > 0:
            prm["mu_small"] = jnp.zeros((1, LANES), F32).at[0, s_vlo:s_vlo + RW_MV_LORA].set(rw_mu_vres[l - 1])
            prm["v2"] = _pad_rows(rw_v2[l - 1], s_vlo, LANES).astype(BF16)
        rw_plan = _rwkv_plan(z3, v_first, prm, rkv_col0=c_rkv, lora_col0=c_lora,
                             small_col0=c_small, width=W, rows=rows)
        hg_plan = _hgrn_plan(z3, lower_bounds[l], hg_norm_w[l], col0=c_hg, width=W, rows=rows)
        outs = _run_plans([ml_plan, rw_plan, hg_plan], batch=B, steps=T // rows, name="mixers")
        y_ml, y_rw, y_hg = outs[0], outs[1], outs[-1]
        if l == 0:
            v_first = outs[2]

        x2 = _branch_mix(x2, mod_all, nw3, y_ml.reshape(N, W), y_rw.reshape(N, W), y_hg.reshape(N, W),
                         wl[:, o_gate:n_in].astype(BF16), w_branch_b, w_out_b, layer=l, tokens_per_batch=T)

        x2 = _ffn(x2, mod_all, nw3, ffn_up_b, ffn_down_b, final_norm_w,
                  layer=l, half=1, tokens_per_batch=T, final_norm=(l == depth - 1))
    return x2.reshape(B, T, D)
```

```python
import functools

import numpy as np
import jax
import jax.numpy as jnp
from jax import lax
from jax.experimental import pallas as pl
from jax.experimental.pallas import tpu as pltpu

F32 = jnp.float32
BF16 = jnp.bfloat16

EPS = 1e-6
NEG_BIG = -1e30
TINY = 1e-30

N_SUB = 3
N_BRANCH = 3
ML_HEADS = 4
ML_CONV = 4
RW_HEAD = 64
RW_LN_EPS = 64e-5
RW_DECAY_LORA = 64
RW_AAA_LORA = 64
RW_MV_LORA = 32
RW_GATE_LORA = 128
HG_EXPAND = 128

LANES = 128
VMEM_LIMIT = 56 * 1024 * 1024

FFN_TM = 1024
FFN_FC = 256
INP_TM = 1024
INP_TN = 1920
MIX_TM = 512
ML_CHUNK = 256
RW_CHUNK = 64
HG_CHUNK = 128
MIXER_ROWS = 256


def _cparams(sem):
    return pltpu.CompilerParams(dimension_semantics=sem, vmem_limit_bytes=VMEM_LIMIT)


def _dot(a, b):
    return jnp.dot(a, b, preferred_element_type=F32)


def _dot_t(a, b):
    return lax.dot_general(a, b, (((1,), (1,)), ((), ())), preferred_element_type=F32)


def _tdot(a, b):
    return lax.dot_general(a, b, (((0,), (0,)), ((), ())), preferred_element_type=F32)


def _split3(x):
    hi = x.astype(BF16)
    r1 = x - hi.astype(F32)
    mid = r1.astype(BF16)
    lo = (r1 - mid.astype(F32)).astype(BF16)
    return hi, mid, lo


def _cumsum_rows(tri, x):
    hi, mid, lo = _split3(x)
    return _dot(tri, hi) + (_dot(tri, mid) + _dot(tri, lo))


def _cumsum_lanes(x, tri):
    hi, mid, lo = _split3(x)
    return _dot(hi, tri) + (_dot(mid, tri) + _dot(lo, tri))


def _sigmoid(x):
    return 1.0 / (1.0 + jnp.exp(-x))


def _gate_sigmoid(x):
    return 0.5 * jnp.tanh(0.5 * x) + 0.5


def _silu(x):
    return x * _gate_sigmoid(x)


def _modulated_norm(x, nw, shift, scale):
    ms = jnp.mean(x * x, axis=-1, keepdims=True)
    return (x * lax.rsqrt(ms + EPS) * nw) * (1.0 + scale) + shift


def _ada_kernel(c_ref, w_ref, b_ref, o_ref):
    cond = _silu(c_ref[...])
    o_ref[0] = _dot(cond, w_ref[0]) + b_ref[0]


def _ada_mod(c, ada_w, ada_b):
    L, D, M = ada_w.shape
    B = c.shape[0]
    tn = M // 8
    return pl.pallas_call(
        _ada_kernel,
        out_shape=jax.ShapeDtypeStruct((L, B, M), F32),
        grid=(L, M // tn),
        in_specs=[
            pl.BlockSpec((B, D), lambda l, j: (0, 0)),
            pl.BlockSpec((1, D, tn), lambda l, j: (l, 0, j)),
            pl.BlockSpec((1, 1, tn), lambda l, j: (l, 0, j)),
        ],
        out_specs=pl.BlockSpec((1, B, tn), lambda l, j: (l, 0, j)),
        compiler_params=_cparams(("arbitrary", "arbitrary")),
        name="ada_mod",
    )(c, ada_w, ada_b.reshape(L, 1, M))


def _ffn_kernel(x_ref, mod_ref, nw_ref, wu_ref, wd_ref, fw_ref, o_ref, acc_ref, *, sub, final_norm):
    x = x_ref[...]
    mod = mod_ref[0, 0]
    shift, scale, gate = (mod[3 * sub + i:3 * sub + i + 1] for i in range(3))
    hb = _modulated_norm(x, nw_ref[0], shift, scale).astype(BF16)
    F = wd_ref.shape[2]
    for c in range(F // FFN_FC):
        cols = slice(c * FFN_FC, (c + 1) * FFN_FC)
        a = _dot(hb, wu_ref[0, 0, :, cols])
        b = _dot(hb, wu_ref[0, 0, :, F + c * FFN_FC:F + (c + 1) * FFN_FC])
        act = (_silu(a) * b).astype(BF16)
        d = _dot(act, wd_ref[0, 0, cols, :])
        if c == 0:
            acc_ref[...] = d
        else:
            acc_ref[...] += d
    y = x + (0.5 * (1.0 + gate)) * acc_ref[...]
    if final_norm:
        ms = jnp.mean(y * y, axis=-1, keepdims=True)
        y = y * lax.rsqrt(ms + EPS) * fw_ref[...]
    o_ref[...] = y


def _ffn(x, mod_all, norm_w, w_up, w_down, fw, *, layer, half, tokens_per_batch, final_norm):
    N, D = x.shape
    sub = 2 * half
    tm = min(FFN_TM, tokens_per_batch)
    bpb = tokens_per_batch // tm
    return pl.pallas_call(
        functools.partial(_ffn_kernel, sub=sub, final_norm=final_norm),
        out_shape=jax.ShapeDtypeStruct((N, D), F32),
        grid=(N // tm,),
        in_specs=[
            pl.BlockSpec((tm, D), lambda i: (i, 0)),
            pl.BlockSpec((1, 1) + mod_all.shape[2:], lambda i: (layer, i // bpb, 0, 0)),
            pl.BlockSpec((1, 1, D), lambda i: (N_SUB * layer + sub, 0, 0)),
            pl.BlockSpec((1, 1) + w_up.shape[2:], lambda i: (layer, half, 0, 0), pipeline_mode=pl.Buffered(1)),
            pl.BlockSpec((1, 1) + w_down.shape[2:], lambda i: (layer, half, 0, 0), pipeline_mode=pl.Buffered(1)),
            pl.BlockSpec((1, D), lambda i: (0, 0)),
        ],
        out_specs=pl.BlockSpec((tm, D), lambda i: (i, 0)),
        scratch_shapes=[pltpu.VMEM((tm, D), F32)],
        compiler_params=_cparams(("arbitrary",)),
        name="ffn_half",
    )(x, mod_all, norm_w, w_up, w_down, fw.reshape(1, D))


def _inproj_kernel(x_ref, mod_ref, nw_ref, w_ref, o_ref, h_ref, *, sub):
    @pl.when(pl.program_id(1) == 0)
    def _():
        mod = mod_ref[0, 0]
        shift, scale = mod[3 * sub:3 * sub + 1], mod[3 * sub + 1:3 * sub + 2]
        h_ref[...] = _modulated_norm(x_ref[...], nw_ref[0], shift, scale).astype(BF16)

    o_ref[...] = _dot(h_ref[...], w_ref[pl.program_id(1)])


def _inproj(x, mod_all, norm_w, w, *, layer, tokens_per_batch):
    N, D = x.shape
    n_tiles = w.shape[0]
    sub = 1
    tm = min(INP_TM, tokens_per_batch)
    bpb = tokens_per_batch // tm
    return pl.pallas_call(
        functools.partial(_inproj_kernel, sub=sub),
        out_shape=jax.ShapeDtypeStruct((N, n_tiles * INP_TN), F32),
        grid=(N // tm, n_tiles),
        in_specs=[
            pl.BlockSpec((tm, D), lambda i, j: (i, 0)),
            pl.BlockSpec((1, 1) + mod_all.shape[2:], lambda i, j: (layer, i // bpb, 0, 0)),
            pl.BlockSpec((1, 1, D), lambda i, j: (N_SUB * layer + sub, 0, 0)),
            pl.BlockSpec(w.shape, lambda i, j: (0, 0, 0), pipeline_mode=pl.Buffered(1)),
        ],
        out_specs=pl.BlockSpec((tm, INP_TN), lambda i, j: (i, j)),
        scratch_shapes=[pltpu.VMEM((tm, D), BF16)],
        compiler_params=_cparams(("arbitrary", "arbitrary")),
        name="mixer_inproj",
    )(x, mod_all, norm_w, w)


def _mix_kernel(x_ref, mod_ref, nw_ref, yml_ref, yrw_ref, yhg_ref, wg_ref, wbr_ref, wout_ref, o_ref,
                *, sub):
    x = x_ref[...]
    D = x.shape[1]
    mod = mod_ref[0, 0]
    shift, scale, gate = (mod[3 * sub + i:3 * sub + i + 1] for i in range(3))
    hb = _modulated_norm(x, nw_ref[0], shift, scale).astype(BF16)
    mixed = None
    for n, y_ref in enumerate((yml_ref, yrw_ref, yhg_ref)):
        gate_pre = _dot(hb, wg_ref[:, n * D:(n + 1) * D])
        term = _gate_sigmoid(gate_pre) * _dot(y_ref[...].astype(BF16), wbr_ref[0, n])
        mixed = term if mixed is None else mixed + term
    o_ref[...] = x + (1.0 + gate) * _dot(mixed.astype(BF16), wout_ref[0])


def _branch_mix(x, mod_all, norm_w, y_ml, y_rw, y_hg, w_gate, w_branch, w_out, *, layer, tokens_per_batch):
    N, D = x.shape
    sub = 1
    tm = min(MIX_TM, tokens_per_batch)
    bpb = tokens_per_batch // tm
    tok = pl.BlockSpec((tm, D), lambda i: (i, 0))
    return pl.pallas_call(
        functools.partial(_mix_kernel, sub=sub),
        out_shape=jax.ShapeDtypeStruct((N, D), F32),
        grid=(N // tm,),
        in_specs=[
            tok,
            pl.BlockSpec((1, 1) + mod_all.shape[2:], lambda i: (layer, i // bpb, 0, 0)),
            pl.BlockSpec((1, 1, D), lambda i: (N_SUB * layer + sub, 0, 0)),
            tok, tok, tok,
            pl.BlockSpec(w_gate.shape, lambda i: (0, 0), pipeline_mode=pl.Buffered(1)),
            pl.BlockSpec((1,) + w_branch.shape[1:], lambda i: (layer, 0, 0, 0), pipeline_mode=pl.Buffered(1)),
            pl.BlockSpec((1,) + w_out.shape[1:], lambda i: (layer, 0, 0), pipeline_mode=pl.Buffered(1)),
        ],
        out_specs=tok,
        compiler_params=_cparams(("arbitrary",)),
        name="branch_mix",
    )(x, mod_all, norm_w, y_ml, y_rw, y_hg, w_gate, w_branch, w_out)


def _log_sigmoid(x):
    return jnp.minimum(x, 0.0) - jnp.log(1.0 + jnp.exp(-jnp.abs(x)))


def _mlstm_kernel(bias_ref, x_ref, op_ref, g_ref, cw_ref, cb_ref, wq_ref, wk_ref,
                  nw_ref, sk_ref, y_ref, c_ref, m_ref, tail_ref):
    TB, W = x_ref.shape[1], x_ref.shape[2]
    L = min(ML_CHUNK, TB)
    H = ML_HEADS
    dh = W // H
    heads = range(H)
    chunks = range(TB // L)
    sls = [slice(h * dh, (h + 1) * dh) for h in heads]
    rws = [slice(j * L, (j + 1) * L) for j in chunks]

    x = x_ref[0]
    cw = cw_ref[...]
    tail = tail_ref[...]
    row8 = lax.broadcasted_iota(jnp.int32, (8, W), 0)
    conv = x * cw[ML_CONV - 1:ML_CONV]
    for s in range(1, ML_CONV):
        xr = pltpu.roll(x, s, 0)
        top = jnp.where(row8 < s, pltpu.roll(tail, s, 0), xr[0:8])
        xs = jnp.concatenate([top, xr[8:]], axis=0)
        conv = conv + xs * cw[ML_CONV - 1 - s:ML_CONV - s]
    tail_ref[...] = x[TB - 8:TB]
    xc = _silu(conv + cb_ref[...])
    xcb = xc.astype(BF16)
    vb = x.astype(BF16)

    g = g_ref[0]
    li_all = jnp.concatenate([g[h:h + 1] + bias_ref[h] for h in heads], axis=0)
    lf_all = _log_sigmoid(jnp.concatenate([g[H + h:H + h + 1] + bias_ref[H + h] for h in heads], axis=0))
    r = lax.broadcasted_iota(jnp.int32, (L, L), 0)
    c = lax.broadcasted_iota(jnp.int32, (L, L), 1)
    causal = r >= c
    tri = (r <= c).astype(BF16)

    lane = lax.broadcasted_iota(jnp.int32, (H, L), 1)
    m_prev = m_ref[...][:, 0:1]
    g_rows, decay, col_rows = [], [], []
    for rw in rws:
        li = li_all[:, rw]
        b_rows = _cumsum_lanes(lf_all[:, rw], tri)
        g_rows.append(li - b_rows)
        cm = g_rows[-1]
        sh = 1
        while sh < L:
            cm = jnp.maximum(cm, jnp.where(lane >= sh, pltpu.roll(cm, sh, 1), NEG_BIG))
            sh *= 2
        mx = jnp.maximum(m_prev, cm)
        b_end = b_rows[:, L - 1:L]
        log_ws = b_end - b_rows + li
        m_new = jnp.maximum(b_end + m_prev, jnp.max(log_ws, axis=-1, keepdims=True))
        decay.append(jnp.exp(b_end + m_prev - m_new))
        col_rows += [-mx, jnp.exp(m_prev - mx), jnp.exp(-(b_rows + mx)), jnp.exp(log_ws - m_new)]
        m_prev = m_new
    m_ref[...] = jnp.broadcast_to(m_prev, (H, LANES))
    cols = jnp.concatenate(col_rows + [jnp.zeros((LANES - 4 * H * len(rws), L), F32)], axis=0).T

    ones_col = (lax.broadcasted_iota(jnp.int32, (TB, LANES), 1) == 0).astype(BF16)
    vaug = [jnp.concatenate([vb[:, sl], ones_col], axis=1) for sl in sls]

    q = [_dot(xcb[:, sl], wq_ref[h]) for h, sl in zip(heads, sls)]
    k = [_dot(xcb[:, sl], wk_ref[h]) * (dh ** -0.5) for h, sl in zip(heads, sls)]
    qb = [t.astype(BF16) for t in q]
    kb = [t.astype(BF16) for t in k]
    units = [(j, h) for j in chunks for h in heads]
    col = lambda u, i: cols[:, (4 * u[0] + i) * H + u[1]:(4 * u[0] + i) * H + u[1] + 1]
    qk = {u: _dot_t(qb[u[1]][rws[u[0]]], kb[u[1]][rws[u[0]]]) for u in units}
    sv, kw = {}, {}
    for u in units:
        s_mat = qk[u] * jnp.where(causal, jnp.exp(col(u, 0) + g_rows[u[0]][u[1]:u[1] + 1]), 0.0)
        sv[u] = _dot(s_mat.astype(BF16), vaug[u[1]][rws[u[0]]])
        kw[u] = (k[u[1]][rws[u[0]]] * col(u, 3)).astype(BF16)

    state = [c_ref[h] for h in heads]
    hs = {}
    for j in chunks:
        qc = [_dot(qb[h][rws[j]], state[h].astype(BF16)) for h in heads]
        for h in heads:
            u = (j, h)
            both = col(u, 1) * qc[h] + sv[u]
            inv = 1.0 / jnp.maximum(jnp.abs(both[:, dh:dh + 1]), col(u, 2))
            hs[u] = both[:, :dh] * inv
        state = [decay[j][h:h + 1] * state[h] + _tdot(kw[(j, h)], vaug[h][rws[j]]) for h in heads]
    for h in heads:
        c_ref[h] = state[h]

    gate = _gate_sigmoid(op_ref[0])
    nw, sk = nw_ref[...], sk_ref[...]
    for h, sl in zip(heads, sls):
        hg = jnp.concatenate([hs[(j, h)] for j in chunks], axis=0) * gate[:, sl]
        hg = hg - jnp.mean(hg, axis=-1, keepdims=True)
        hn = hg * lax.rsqrt(jnp.mean(hg * hg, axis=-1, keepdims=True) + EPS)
        y_ref[0, :, sl] = hn * nw[:, sl] + sk[:, sl] * xc[:, sl]


def _mlstm_plan(z3, gates, bias, conv_w, conv_b, wq, wk, norm_w, skip, *, width, rows):
    B, T, _ = z3.shape
    dh = width // ML_HEADS
    vec = lambda b, t: (0, 0)
    return dict(
        body=_mlstm_kernel,
        args=[bias, z3, z3, gates, conv_w, conv_b.reshape(1, width), wq.astype(BF16),
              wk.astype(BF16), norm_w.reshape(1, width), skip.reshape(1, width)],
        in_specs=[
            pl.BlockSpec(memory_space=pltpu.SMEM),
            pl.BlockSpec((1, rows, width), lambda b, t: (b, t, 0)),
            pl.BlockSpec((1, rows, width), lambda b, t: (b, t, 1)),
            pl.BlockSpec((1, 2 * ML_HEADS, rows), lambda b, t: (b, 0, t)),
            pl.BlockSpec((ML_CONV, width), vec),
            pl.BlockSpec((1, width), vec),
            pl.BlockSpec((ML_HEADS, dh, dh), lambda b, t: (0, 0, 0)),
            pl.BlockSpec((ML_HEADS, dh, dh), lambda b, t: (0, 0, 0)),
            pl.BlockSpec((1, width), vec),
            pl.BlockSpec((1, width), vec),
        ],
        out_shape=[jax.ShapeDtypeStruct((B, T, width), F32)],
        out_specs=[pl.BlockSpec((1, rows, width), lambda b, t: (b, t, 0))],
        scratch=[
            pltpu.VMEM((ML_HEADS, dh, dh + LANES), F32),
            pltpu.VMEM((ML_HEADS, LANES), F32),
            pltpu.VMEM((8, width), F32),
        ],
    )


def _zero_refs(refs):
    for ref in refs:
        ref[...] = jnp.zeros_like(ref)


def _plans_kernel(*refs, plans):
    n_in = [len(p["args"]) for p in plans]
    n_out = [len(p["out_shape"]) for p in plans]
    n_scr = [len(p["scratch"]) for p in plans]
    ins, outs, scr = refs[:sum(n_in)], refs[sum(n_in):sum(n_in) + sum(n_out)], refs[sum(n_in) + sum(n_out):]
    pl.when(pl.program_id(1) == 0)(functools.partial(_zero_refs, scr))
    i = o = s = 0
    for p, ni, no, ns in zip(plans, n_in, n_out, n_scr):
        p["body"](*ins[i:i + ni], *outs[o:o + no], *scr[s:s + ns])
        i, o, s = i + ni, o + no, s + ns


def _run_plans(plans, *, batch, steps, name):
    static = [dict(body=p["body"], args=[None] * len(p["args"]), out_shape=[None] * len(p["out_shape"]),
                   scratch=[None] * len(p["scratch"])) for p in plans]
    return pl.pallas_call(
        functools.partial(_plans_kernel, plans=static),
        out_shape=[s for p in plans for s in p["out_shape"]],
        grid=(batch, steps),
        in_specs=[s for p in plans for s in p["in_specs"]],
        out_specs=[s for p in plans for s in p["out_specs"]],
        scratch_shapes=[s for p in plans for s in p["scratch"]],
        compiler_params=_cparams(("arbitrary", "arbitrary")),
        name=name,
    )(*[a for p in plans for a in p["args"]])


def _midrow_broadcast(b, m):
    n, w = b.shape
    row = lambda i: b[i:i + 1]
    if m >= 8:
        return jnp.concatenate(
            [jnp.broadcast_to(row(p * 2 * m + m), (2 * m, w)) for p in range(n // (2 * m))], axis=0)
    pick = lambda off: jnp.concatenate(
        [jnp.broadcast_to(row(8 * g + off), (8, w)) for g in range(n // 8)], axis=0)
    if m == 4:
        return pick(4)
    sub = lax.broadcasted_iota(jnp.int32, (n, LANES), 0) % 8
    tiles = [slice(g * LANES, (g + 1) * LANES) for g in range(w // LANES)]
    if m == 2:
        lo, hi = pick(2), pick(6)
        return jnp.concatenate([jnp.where(sub < 4, lo[:, t], hi[:, t]) for t in tiles], axis=1)
    p1, p3, p5, p7 = pick(1), pick(3), pick(5), pick(7)
    return jnp.concatenate(
        [jnp.where(sub < 2, p1[:, t], jnp.where(sub < 4, p3[:, t], jnp.where(sub < 6, p5[:, t], p7[:, t])))
         for t in tiles], axis=1)


def _hgrn_kernel(q_ref, f_ref, i_ref, g_ref, lb_ref, nw_ref, y_ref, s_ref):
    TB, W = q_ref.shape[1], q_ref.shape[2]
    C = min(HG_CHUNK, TB)
    H = W // HG_EXPAND
    chunks = range(TB // C)
    rws = [slice(j * C, (j + 1) * C) for j in chunks]

    lb = lb_ref[...]
    sig = _sigmoid(f_ref[0])
    q = _silu(q_ref[0])
    k = (1.0 - lb) * (1.0 - sig)
    log_f = jnp.log(jnp.maximum(lb + (1.0 - lb) * sig, TINY))
    rt = lax.broadcasted_iota(jnp.int32, (TB, TB), 0)
    ct = lax.broadcasted_iota(jnp.int32, (TB, TB), 1)
    b = _cumsum_rows(((rt >= ct) & ((rt // C) == (ct // C))).astype(BF16), log_f)
    v = i_ref[0]
    vb = v.astype(BF16)

    r = lax.broadcasted_iota(jnp.int32, (C, C), 0)
    c = lax.broadcasted_iota(jnp.int32, (C, C), 1)
    rowi = lax.broadcasted_iota(jnp.int32, (TB, HG_EXPAND), 0)
    qb, kb = q.astype(BF16), k.astype(BF16)
    heads = range(H)
    sls = [slice(h * HG_EXPAND, (h + 1) * HG_EXPAND) for h in heads]
    units = [(j, h) for j in chunks for h in heads]
    a = {u: jnp.zeros((C, C), F32) for u in units}
    m = 1
    while m < C:
        e = jnp.exp(-jnp.abs(b - _midrow_broadcast(b, m)))
        second = (rowi % (2 * m)) >= m
        z_l = jnp.concatenate([jnp.where(second, qb[:, sl], kb[:, sl]) for sl in sls], axis=1) * e.astype(BF16)
        mask = ((r // (2 * m)) == (c // (2 * m))) & ((r % (2 * m)) >= m) & ((c % (2 * m)) < m)
        for j, h in units:
            zu = z_l[rws[j], sls[h]]
            a[(j, h)] = jnp.where(mask, _dot_t(zu, zu), a[(j, h)])
        m *= 2

    qe = (q * jnp.exp(b)).astype(BF16)
    b_ends = [b[(j + 1) * C - 1:(j + 1) * C] for j in chunks]
    b_end_rows = jnp.concatenate([jnp.broadcast_to(b_ends[j], (C, W)) for j in chunks], axis=0)
    kd = (k * jnp.exp(b_end_rows - b)).astype(BF16)
    diag = q * k
    intra = {(j, h): _dot(a[(j, h)].astype(BF16), vb[rws[j], sls[h]])
             + jnp.sum(diag[rws[j], sls[h]], axis=-1, keepdims=True) * v[rws[j], sls[h]] for j, h in units}
    kv = {(j, h): _tdot(vb[rws[j], sls[h]], kd[rws[j], sls[h]]) for j, h in units}
    st = [s_ref[h] for h in heads]
    o_rows = []
    for j in chunks:
        inter = [_dot_t(qe[rws[j], sls[h]], st[h].astype(BF16)) for h in heads]
        g_end = jnp.exp(b_ends[j])
        st = [st[h] * g_end[:, sls[h]] + kv[(j, h)] for h in heads]
        o_rows.append([inter[h] + intra[(j, h)] for h in heads])
    for h in heads:
        s_ref[h] = st[h]
    gate = _gate_sigmoid(g_ref[0])
    nw = nw_ref[...]
    for h, sl in zip(heads, sls):
        o = jnp.concatenate([o_rows[j][h] for j in chunks], axis=0)
        on = o * lax.rsqrt(jnp.mean(o * o, axis=-1, keepdims=True) + EPS)
        y_ref[0, :, sl] = on * nw[:, sl] * gate[:, sl]


def _hgrn_plan(z3, lower_bound, norm_w, *, col0, width, rows):
    B, T, _ = z3.shape
    H = width // HG_EXPAND
    blk0 = col0 // width
    spec = lambda j: pl.BlockSpec((1, rows, width), lambda b, t: (b, t, blk0 + j))
    vec = pl.BlockSpec((1, width), lambda b, t: (0, 0))
    return dict(
        body=_hgrn_kernel,
        args=[z3, z3, z3, z3, lower_bound.reshape(1, width), norm_w.reshape(1, width)],
        in_specs=[spec(0), spec(1), spec(2), spec(3), vec, vec],
        out_shape=[jax.ShapeDtypeStruct((B, T, width), F32)],
        out_specs=[pl.BlockSpec((1, rows, width), lambda b, t: (b, t, 0))],
        scratch=[pltpu.VMEM((H, HG_EXPAND, HG_EXPAND), F32)],
    )


def _token_shift(z, mu, last_ref):
    n, w = z.shape
    rolled = pltpu.roll(z, 1, 0)
    row8 = lax.broadcasted_iota(jnp.int32, (8, w), 0)
    top = jnp.where(row8 == 0, last_ref[7:8], rolled[0:8])
    prev = jnp.concatenate([top, rolled[8:]], axis=0)
    last_ref[...] = z[n - 8:n]
    return z + mu * (prev - z)


def _head_sum(x, ones_bd, two_pass=False):
    n, w = x.shape
    slab = ones_bd.shape[0]
    xs = jnp.concatenate([x[:, g * slab:(g + 1) * slab] for g in range(w // slab)], axis=0)
    hi = xs.astype(BF16)
    s = _dot(hi, ones_bd)
    if two_pass:
        s = s + _dot((xs - hi.astype(F32)).astype(BF16), ones_bd)
    return jnp.concatenate([s[g * n:(g + 1) * n] for g in range(w // slab)], axis=1)


def _softplus(x):
    return jnp.maximum(x, 0.0) + jnp.log(1.0 + jnp.exp(-jnp.abs(x)))


def _rwkv_kernel(*refs, has_vres):
    it = iter(refs)
    r_ref, k_ref, v_ref, lo_ref = next(it), next(it), next(it), next(it)
    sm_ref, vf_ref = (next(it), next(it)) if has_vres else (None, None)
    mu_ref, mulo_ref, vec_ref, w2_ref, a2_ref, g2_ref = (next(it) for _ in range(6))
    musm_ref, v2_ref = (next(it), next(it)) if has_vres else (None, None)
    y_ref = next(it)
    vfo_ref = None if has_vres else next(it)
    s_ref, pr_ref, pk_ref, pv_ref, plo_ref = (next(it) for _ in range(5))
    psm_ref = next(it) if has_vres else None

    TB, W = r_ref.shape[1], r_ref.shape[2]
    C = min(RW_CHUNK, TB)
    P = W // LANES

    vec = vec_ref[...]
    w0, a0, k_k, k_a, ln_w, ln_b, r_k, v0 = (vec[i:i + 1] for i in range(8))
    mu = mu_ref[...]
    r = _token_shift(r_ref[0], mu[0:1], pr_ref)
    k = _token_shift(k_ref[0], mu[1:2], pk_ref)
    v = _token_shift(v_ref[0], mu[2:3], pv_ref)
    lo = _token_shift(lo_ref[0], mulo_ref[...], plo_ref)
    lo_wa, lo_g = lo[:, :LANES], lo[:, LANES:]

    w_raw = -_softplus(-(w0 + _dot(jnp.tanh(lo_wa).astype(BF16), w2_ref[...]))) - 0.5
    lw = -jnp.exp(w_raw)
    a = _gate_sigmoid(a0 + _dot(lo_wa.astype(BF16), a2_ref[...]))
    g = _dot(_gate_sigmoid(lo_g).astype(BF16), g2_ref[...])
    if has_vres:
        sm = _token_shift(sm_ref[0], musm_ref[...], psm_ref)
        v = v + (vf_ref[0] - v) * _gate_sigmoid(v0 + _dot(sm.astype(BF16), v2_ref[...]))
    else:
        vfo_ref[0] = v

    rr2 =lax.broadcasted_iota(jnp.int32, (2 * LANES, 2 * LANES), 0)
    cc2 = lax.broadcasted_iota(jnp.int32, (2 * LANES, 2 * LANES), 1)
    ones_bd = ((rr2 // RW_HEAD) == (cc2 // RW_HEAD)).astype(BF16)
    rr = lax.broadcasted_iota(jnp.int32, (LANES, LANES), 0)
    cc = lax.broadcasted_iota(jnp.int32, (LANES, LANES), 1)

    kk = k * k_k
    kmod = k * (1.0 + (a - 1.0) * k_a)
    stats = _head_sum(jnp.concatenate([kk * kk, r * kmod * r_k], axis=0), ones_bd)
    kk = kk * jnp.minimum(lax.rsqrt(stats[:TB]), 1e12)
    bonus_dot = stats[TB:]
    ahat = kk * a

    tr = lax.broadcasted_iota(jnp.int32, (TB, TB), 0)
    tc = lax.broadcasted_iota(jnp.int32, (TB, TB), 1)
    cum = _cumsum_rows(((tr >= tc) & ((tr // C) == (tc // C))).astype(BF16), lw)
    e_inc = jnp.exp(cum)
    e_inv = jnp.exp(-cum)
    r_dec = r * e_inc
    kk_dec = kk * jnp.exp(cum - lw)
    a_inv = ahat * e_inv
    k_inv = kmod * e_inv

    lane =lax.broadcasted_iota(jnp.int32, (C, LANES), 1)
    first = lane < RW_HEAD

    def stack(zp):
        return jnp.concatenate([jnp.where(first, zp, 0.0), jnp.where(first, 0.0, zp)], axis=0).astype(BF16)

    strict = rr > cc
    incl = rr >= cc
    eye = (rr == cc).astype(F32)
    n_sq = int(np.log2(C)) - 1
    n2 = 2 * C
    units = [(j, p) for j in range(TB // C) for p in range(P)]
    cut = lambda t, u: t[u[0] * C:(u[0] + 1) * C, u[1] * LANES:(u[1] + 1) * LANES]
    kks = {u: stack(cut(kk_dec, u)) for u in units}
    rs = {u: stack(cut(r_dec, u)) for u in units}
    vs = {u: stack(cut(v, u)) for u in units}
    ak = {u: jnp.concatenate([stack(cut(a_inv, u)), stack(cut(k_inv, u))], axis=0) for u in units}
    sc = {u: _dot_t(jnp.concatenate([kks[u], rs[u]], axis=0), ak[u]) for u in units}
    l_a = {u: jnp.where(strict, sc[u][:n2, :n2], 0.0) for u in units}
    lkv = {u: _dot(jnp.where(strict, sc[u][:n2, n2:], 0.0).astype(BF16), vs[u]) for u in units}
    a_cat = {u: jnp.where(jnp.concatenate([incl, incl], axis=1), sc[u][n2:], 0.0).astype(BF16) for u in units}
    t_inv = {u: eye - l_a[u] for u in units}
    pw = {u: _dot(l_a[u].astype(BF16), l_a[u].astype(BF16)).astype(BF16) for u in units}
    for _ in range(n_sq - 1):
        both = {u: _dot(pw[u], jnp.concatenate([t_inv[u].astype(BF16), pw[u]], axis=1)) for u in units}
        t_inv = {u: t_inv[u] + both[u][:, :n2] for u in units}
        pw = {u: both[u][:, n2:].astype(BF16) for u in units}
    t_inv = {u: t_inv[u] + _dot(pw[u], t_inv[u].astype(BF16)) for u in units}
    rhs = {u: jnp.concatenate([-kks[u].astype(F32), -lkv[u]], axis=1).astype(BF16) for u in units}
    wv = {u: _dot(t_inv[u].astype(BF16), rhs[u]) for u in units}

    st = [s_ref[p] for p in range(P)]
    y_rows = []
    for j in range(TB // C):
        us = [(j, p) for p in range(P)]
        wr = [_dot_t(jnp.concatenate([wv[u][:, :LANES].astype(BF16), rs[u]], axis=0), st[u[1]].astype(BF16))
              for u in us]
        uv = [jnp.concatenate([(wr[p][:n2] + wv[u][:, LANES:]).astype(BF16), vs[u]], axis=0)
              for p, u in enumerate(us)]
        yst = [wr[p][n2:] + _dot(a_cat[u], uv[p]) for p, u in enumerate(us)]
        gamma_end = e_inc[(j + 1) * C - 1:(j + 1) * C]
        st = [(st[p] + _tdot(uv[p], ak[u])) * gamma_end[:, p * LANES:(p + 1) * LANES]
              for p, u in enumerate(us)]
        y_rows.append(jnp.concatenate([yst[p][:C] + yst[p][C:] for p in range(P)], axis=1))
    for p in range(P):
        s_ref[p] = st[p]
    y = jnp.concatenate(y_rows, axis=0)

    inv_n = 1.0 / RW_HEAD
    yc = y - _head_sum(y, ones_bd, two_pass=True) * inv_n
    yn = yc * lax.rsqrt(_head_sum(yc * yc, ones_bd) * inv_n + RW_LN_EPS) * ln_w + ln_b
    y_ref[0] = (yn + bonus_dot * v) * g


def _rwkv_plan(z3, v_first, prm, *, rkv_col0, lora_col0, small_col0, width, rows):
    B, T, _ = z3.shape
    C = rows
    has_vres = v_first is not None
    lora_w = RW_DECAY_LORA + RW_AAA_LORA + RW_GATE_LORA
    tok = lambda wdt, col0, j=0: pl.BlockSpec((1, C, wdt), lambda b, t: (b, t, col0 // wdt + j))
    full = lambda arr: pl.BlockSpec(arr.shape, lambda b, t: (0,) * arr.ndim)
    tok_out = pl.BlockSpec((1, C, width), lambda b, t: (b, t, 0))

    args = [z3, z3, z3, z3]
    specs = [tok(width, rkv_col0, 0), tok(width, rkv_col0, 1), tok(width, rkv_col0, 2), tok(lora_w, lora_col0)]
    if has_vres:
        args += [z3, v_first]
        specs += [tok(LANES, small_col0), tok_out]
    weights = [prm["mu_rkv"], prm["mu_lora"], prm["vecs"], prm["w2"], prm["a2"], prm["g2"]]
    if has_vres:
        weights += [prm["mu_small"], prm["v2"]]
    args += weights
    specs += [full(wt) for wt in weights]

    y_shape = jax.ShapeDtypeStruct((B, T, width), F32)
    scratch = [pltpu.VMEM((width // LANES, LANES, LANES), F32)]
    scratch += [pltpu.VMEM((8, width), F32)] * 3 + [pltpu.VMEM((8, lora_w), F32)]
    if has_vres:
        scratch += [pltpu.VMEM((8, LANES), F32)]
    n_out = 1 if has_vres else 2
    return dict(
        body=functools.partial(_rwkv_kernel, has_vres=has_vres),
        args=args, in_specs=specs,
        out_shape=[y_shape] * n_out, out_specs=[tok_out] * n_out,
        scratch=scratch,
    )


def _pad_rows(w, row0, rows):
    return jnp.zeros((rows, w.shape[1]), w.dtype).at[row0:row0 + w.shape[0]].set(w)


def kernel(x, c, norm_w, final_norm_w, ada_w, ada_b, ffn_up, ffn_down, w_in, w_in_vres,
           ml_conv_w, ml_conv_b, ml_wq, ml_wk, ml_i_b, ml_f_b, ml_norm_w, ml_skip,
           rw_mu, rw_mu_vres, rw_w0, rw_w2, rw_a0, rw_a2, rw_v0, rw_v2, rw_g2,
           rw_k_k, rw_k_a, rw_r_k, rw_ln_w, rw_ln_b, hg_lb_logits, hg_norm_w,
           w_branch, w_out):
    B, T, D = x.shape
    depth = norm_w.shape[0]
    N = B * T
    W = D
    lora_w = RW_DECAY_LORA + RW_AAA_LORA + RW_GATE_LORA

    o_mlx, o_mlo, o_mli, o_mlf = 0, W, 2 * W, 2 * W + ML_HEADS
    o_rw = 2 * W + 2 * ML_HEADS
    o_hg = o_rw + 3 * W + lora_w
    o_gate = o_hg + 4 * W
    n_in = o_gate + N_BRANCH * D
    c_rkv, c_hg = 2 * W, 5 * W
    c_lora = c_hg + 4 * W
    c_small = c_lora + lora_w
    n_cols = c_small + LANES
    s_vlo = 2 * ML_HEADS

    lb_soft = jax.nn.softmax(hg_lb_logits.astype(F32), axis=0)
    lower_bounds = jnp.cumsum(lb_soft, axis=0) - lb_soft[0]

    mod_all = _ada_mod(c, ada_w, ada_b).reshape(depth, B, N_SUB * 3, D)
    nw3 = norm_w.reshape(depth * N_SUB, 1, D)
    ffn_up_b, ffn_down_b = ffn_up.astype(BF16), ffn_down.astype(BF16)
    w_branch_b, w_out_b = w_branch.astype(BF16), w_out.astype(BF16)
    x2 = x.reshape(N, D)
    v_first = None
    for l in range(depth):
        wl = w_in[l]
        small = jnp.concatenate([wl[:, o_mli:o_mli + 2 * ML_HEADS]]
                                + ([w_in_vres[l - 1]] if l > 0 else []), axis=1)
        small = jnp.pad(small, ((0, 0), (0, LANES - small.shape[1])))
        w_cat = jnp.concatenate([
            wl[:, o_mlx:o_mlx + 2 * W], wl[:, o_rw:o_rw + 3 * W], wl[:, o_hg:o_hg + 4 * W],
            wl[:, o_rw + 3 * W:o_rw + 3 * W + lora_w], small], axis=1).astype(BF16)
        w_cat = w_cat.reshape(D, n_cols // INP_TN, INP_TN).transpose(1, 0, 2)

        x2 = _ffn(x2, mod_all, nw3, ffn_up_b, ffn_down_b, final_norm_w,
                  layer=l, half=0, tokens_per_batch=T, final_norm=False)

        z = _inproj(x2, mod_all, nw3, w_cat, layer=l, tokens_per_batch=T)
        z3 = z.reshape(B, T, n_cols)

        rows = min(MIXER_ROWS, T)
        gates = z3[:, :, c_small:c_small + 2 * ML_HEADS].transpose(0, 2, 1)
        ml_plan = _mlstm_plan(z3, gates, jnp.concatenate([ml_i_b[l], ml_f_b[l]]), ml_conv_w[l], ml_conv_b[l],
                              ml_wq[l], ml_wk[l], ml_norm_w[l], ml_skip[l], width=W, rows=rows)

        mu = rw_mu[l]
        prm = {
            "mu_rkv": mu[:3 * W].reshape(3, W),
            "mu_lora": mu[3 * W:].reshape(1, lora_w),
            "vecs": jnp.stack([rw_w0[l], rw_a0[l], rw_k_k[l], rw_k_a[l], rw_ln_w[l], rw_ln_b[l],
                               rw_r_k[l].reshape(W), rw_v0[l - 1] if l > 0 else jnp.zeros((W,), F32)]),
            "w2": _pad_rows(rw_w2[l], 0, LANES).astype(BF16),
            "a2": _pad_rows(rw_a2[l], RW_DECAY_LORA, LANES).astype(BF16),
            "g2": rw_g2[l].astype(BF16),
        }
        if l > 0:
            prm["mu_small"] = jnp.zeros((1, LANES), F32).at[0, s_vlo:s_vlo + RW_MV_LORA].set(rw_mu_vres[l - 1])
            prm["v2"] = _pad_rows(rw_v2[l - 1], s_vlo, LANES).astype(BF16)
        rw_plan = _rwkv_plan(z3, v_first, prm, rkv_col0=c_rkv, lora_col0=c_lora,
                             small_col0=c_small, width=W, rows=rows)
        hg_plan = _hgrn_plan(z3, lower_bounds[l], hg_norm_w[l], col0=c_hg, width=W, rows=rows)
        outs = _run_plans([ml_plan, rw_plan, hg_plan], batch=B, steps=T // rows, name="mixers")
        y_ml, y_rw, y_hg = outs[0], outs[1], outs[-1]
        if l == 0:
            v_first = outs[2]

        x2 = _branch_mix(x2, mod_all, nw3, y_ml.reshape(N, W), y_rw.reshape(N, W), y_hg.reshape(N, W),
                         wl[:, o_gate:n_in].astype(BF16), w_branch_b, w_out_b, layer=l, tokens_per_batch=T)

        x2 = _ffn(x2, mod_all, nw3, ffn_up_b, ffn_down_b, final_norm_w,
                  layer=l, half=1, tokens_per_batch=T, final_norm=(l == depth - 1))
    return x2.reshape(B, T, D)
```
